```python
import jax, jax.numpy as jnp
from jax import lax
import numpy as np

D_MODEL = 1024
BATCH = 16
SEQ = 2048
DEPTH = 1

HEAD_DIM = 64
N_ATTN_HEADS = 8
ATTN_WIDTH = N_ATTN_HEADS * HEAD_DIM
N_SGU_GROUPS = 8
SGU_GROUP_DIM = 64
SGU_WIDTH = N_SGU_GROUPS * SGU_GROUP_DIM
MIX_WIDTH = ATTN_WIDTH + SGU_WIDTH
IN_WIDTH = 3 * ATTN_WIDTH + 2 * SGU_WIDTH
DILATION_PATTERNS = ((128, 1), (512, 4), (2048, 16))
SGU_CHUNK = 128
ROPE_THETA = 500000.0
ROT_DIM = HEAD_DIM // 4
N_GROUPS = 4
EXPERTS_PER_GROUP = 8
TOP_K = 2
EXPERT_FF = 256
LN_EPS = 1e-5
NEG_INF = -1e30
DEEPNORM_ALPHA = (2 * DEPTH) ** 0.25
DEEPNORM_BETA = (8 * DEPTH) ** -0.25

kernel_name = "hybrid_dilated_attn_sgu_hmoe_deepnorm"


def layer_norm(x, g, b):
    xf = x.astype(jnp.float32)
    mu = jnp.mean(xf, -1, keepdims=True)
    xc = xf - mu
    var = jnp.mean(xc * xc, -1, keepdims=True)
    return xc * lax.rsqrt(var + LN_EPS) * g.astype(jnp.float32) + b.astype(jnp.float32)


def rope_tables(positions):
    inv = ROPE_THETA ** (-jnp.arange(0, ROT_DIM, 2, dtype=jnp.float32) / ROT_DIM)
    ang = positions.astype(jnp.float32)[..., None] * inv
    return jnp.cos(ang)[:, :, None, :], jnp.sin(ang)[:, :, None, :]


def apply_partial_rope(t, cos, sin):
    half = ROT_DIM // 2
    t1, t2 = t[..., :half], t[..., half:ROT_DIM]
    return jnp.concatenate([t1 * cos - t2 * sin, t2 * cos + t1 * sin, t[..., ROT_DIM:]], -1)


def dilated_window_attention(q, k, v, window, dilation):
    B, S, H, Dh = q.shape
    d = dilation
    L = S // d
    w_sub = window // d
    blk = min(w_sub, L)
    nb = -(-L // blk)
    Lp = nb * blk

    def to_blocks(t):
        t = t.reshape(B, L, d, H, Dh)
        t = jnp.pad(t, ((0, 0), (0, Lp - L), (0, 0), (0, 0), (0, 0)))
        return t.reshape(B, nb, blk, d, H, Dh)

    qb, kb, vb = to_blocks(q), to_blocks(k), to_blocks(v)

    def with_prev(t):
        prev = jnp.pad(t, ((0, 0), (1, 0), (0, 0), (0, 0), (0, 0), (0, 0)))[:, :nb]
        return jnp.concatenate([prev, t], axis=2)

    kk, vv = with_prev(kb), with_prev(vb)
    s = jnp.einsum('bnqrhe,bnkrhe->bnrhqk', qb, kk) * (Dh ** -0.5)
    qi = jnp.arange(blk)[:, None]
    kj = jnp.arange(2 * blk)[None, :]
    dist = qi + blk - kj
    key_pos = jnp.arange(nb)[:, None, None] * blk - blk + kj[None]
    valid = (dist >= 0) & (dist <= w_sub) & (key_pos >= 0)
    valid = valid[None, :, None, None]
    s = jnp.where(valid, s, NEG_INF)
    m = jnp.max(s, -1)
    p = jnp.where(valid, jnp.exp(s - m[..., None]), 0.0)
    l = jnp.sum(p, -1)
    m_t = jnp.moveaxis(m, -1, 2)
    l_t = jnp.moveaxis(l, -1, 2)
    o = jnp.einsum('bnrhqk,bnkrhe->bnqrhe', p, vv) / l_t[..., None]
    o = o.reshape(B, Lp, d, H, Dh)[:, :L].reshape(B, S, H, Dh)
    m_t = m_t.reshape(B, Lp, d, H)[:, :L].reshape(B, S, H)
    l_t = l_t.reshape(B, Lp, d, H)[:, :L].reshape(B, S, H)
    return o, m_t, l_t


def dilated_mixture_attention(q, k, v):
    res = [dilated_window_attention(q, k, v, w, d) for (w, d) in DILATION_PATTERNS]
    m_all = jnp.stack([r[1] for r in res], 0)
    l_all = jnp.stack([r[2] for r in res], 0)
    o_all = jnp.stack([r[0] for r in res], 0)
    wts = l_all * jnp.exp(m_all - jnp.max(m_all, 0, keepdims=True))
    o = jnp.sum(wts[..., None] * o_all, 0) / jnp.sum(wts, 0)[..., None]
    B, S, H, Dh = q.shape
    return o.reshape(B, S, H * Dh)


def spatial_gating(u, vs, ln_g, ln_b, w_spatial, b_spatial):
    B, S, _ = u.shape
    u = jax.nn.gelu(u.astype(jnp.float32), approximate=False)
    vs = layer_norm(jax.nn.gelu(vs.astype(jnp.float32), approximate=False), ln_g, ln_b)
    vc = vs.reshape(B, S // SGU_CHUNK, SGU_CHUNK, N_SGU_GROUPS, SGU_GROUP_DIM)
    causal = jnp.tril(jnp.ones((SGU_CHUNK, SGU_CHUNK), jnp.float32))
    w = w_spatial.astype(jnp.float32) * causal
    z = jnp.einsum('gts,bcsge->bctge', w, vc)
    z = z + b_spatial.astype(jnp.float32).T[None, None, :, :, None]
    return u * z.reshape(B, S, SGU_WIDTH)


def hierarchical_moe(h, w_group, b_group, w_expert, b_expert, w_gate_up, w_down):
    B, S, D = h.shape
    t = h.reshape(B * S, D)
    g_logits = (t @ w_group + b_group).astype(jnp.float32)
    p_group = jax.nn.softmax(g_logits, -1)
    g_onehot = jax.nn.one_hot(jnp.argmax(g_logits, -1), N_GROUPS, dtype=jnp.float32)
    p_sel = jnp.sum(p_group * g_onehot, -1)
    e_logits = (jnp.einsum('td,gde->tge', t, w_expert) + b_expert).astype(jnp.float32)
    e_sel = jnp.einsum('tge,tg->te', e_logits, g_onehot)
    top_vals, top_idx = lax.top_k(e_sel, TOP_K)
    top_w = jax.nn.softmax(top_vals, -1)
    e_gate = jnp.einsum('tk,tke->te', top_w,
                        jax.nn.one_hot(top_idx, EXPERTS_PER_GROUP, dtype=jnp.float32))
    gate = g_onehot[:, :, None] * e_gate[:, None, :] * p_sel[:, None, None]
    y = jnp.zeros((B * S, D), jnp.float32)
    for g in range(N_GROUPS):
        gu = jnp.einsum('td,edf->tef', t, w_gate_up[g]).astype(jnp.float32)
        a, b = gu[..., :EXPERT_FF], gu[..., EXPERT_FF:]
        act = jax.nn.silu(a) * b * gate[:, g, :, None]
        y = y + jnp.einsum('tef,efd->td', act, w_down[g].astype(jnp.float32))
    return y.reshape(B, S, D)


def setup_inputs(seed: int = 0) -> dict:
    key = jax.random.key(seed)
    ks = jax.random.split(key, 18)
    f32 = jnp.float32
    x = jax.random.normal(ks[0], (BATCH, SEQ, D_MODEL), f32)
    positions = (jax.random.randint(ks[1], (BATCH, 1), 0, 4096, dtype=jnp.int32)
                 + jnp.arange(SEQ, dtype=jnp.int32)[None, :])
    col_scale = jnp.concatenate([jnp.ones((2 * ATTN_WIDTH,), f32),
                                 jnp.full((ATTN_WIDTH,), DEEPNORM_BETA, f32),
                                 jnp.ones((2 * SGU_WIDTH,), f32)])
    w_in = jax.random.normal(ks[2], (DEPTH, D_MODEL, IN_WIDTH), f32) * (D_MODEL ** -0.5) * col_scale
    sgu_ln_g = 1.0 + 0.1 * jax.random.normal(ks[3], (DEPTH, SGU_WIDTH), f32)
    sgu_ln_b = 0.02 * jax.random.normal(ks[4], (DEPTH, SGU_WIDTH), f32)
    w_spatial = jax.random.normal(ks[5], (DEPTH, N_SGU_GROUPS, SGU_CHUNK, SGU_CHUNK), f32) * (SGU_CHUNK ** -0.5)
    b_spatial = 1.0 + 0.1 * jax.random.normal(ks[6], (DEPTH, N_SGU_GROUPS, SGU_CHUNK), f32)
    w_out = jax.random.normal(ks[7], (DEPTH, MIX_WIDTH, D_MODEL), f32) * (MIX_WIDTH ** -0.5) * DEEPNORM_BETA
    ln1_g = 1.0 + 0.1 * jax.random.normal(ks[8], (DEPTH, D_MODEL), f32)
    ln1_b = 0.02 * jax.random.normal(ks[9], (DEPTH, D_MODEL), f32)
    w_group = jax.random.normal(ks[10], (DEPTH, D_MODEL, N_GROUPS), f32) * (D_MODEL ** -0.5)
    b_group = 0.01 * jax.random.normal(ks[11], (DEPTH, N_GROUPS), f32)
    w_expert = jax.random.normal(ks[12], (DEPTH, N_GROUPS, D_MODEL, EXPERTS_PER_GROUP), f32) * (D_MODEL ** -0.5)
    b_expert = 0.01 * jax.random.normal(ks[13], (DEPTH, N_GROUPS, EXPERTS_PER_GROUP), f32)
    w_gate_up = jax.random.normal(ks[14], (DEPTH, N_GROUPS, EXPERTS_PER_GROUP, D_MODEL, 2 * EXPERT_FF), f32) * (D_MODEL ** -0.5)
    w_down = jax.random.normal(ks[15], (DEPTH, N_GROUPS, EXPERTS_PER_GROUP, EXPERT_FF, D_MODEL), f32) * (EXPERT_FF ** -0.5) * DEEPNORM_BETA
    ln2_g = 1.0 + 0.1 * jax.random.normal(ks[16], (DEPTH, D_MODEL), f32)
    ln2_b = 0.02 * jax.random.normal(ks[17], (DEPTH, D_MODEL), f32)
    return {"x": x, "positions": positions, "w_in": w_in, "sgu_ln_g": sgu_ln_g,
            "sgu_ln_b": sgu_ln_b, "w_spatial": w_spatial, "b_spatial": b_spatial,
            "w_out": w_out, "ln1_g": ln1_g, "ln1_b": ln1_b, "w_group": w_group,
            "b_group": b_group, "w_expert": w_expert, "b_expert": b_expert,
            "w_gate_up": w_gate_up, "w_down": w_down, "ln2_g": ln2_g, "ln2_b": ln2_b}


def reference(x, positions, w_in, sgu_ln_g, sgu_ln_b, w_spatial, b_spatial, w_out,
              ln1_g, ln1_b, w_group, b_group, w_expert, b_expert, w_gate_up, w_down,
              ln2_g, ln2_b):
    B, S, _ = x.shape
    cos, sin = rope_tables(positions)
    h = x.astype(jnp.float32)
    for layer in range(DEPTH):
        proj = h @ w_in[layer]
        a0 = ATTN_WIDTH
        q = proj[..., 0:a0].reshape(B, S, N_ATTN_HEADS, HEAD_DIM).astype(jnp.float32)
        k = proj[..., a0:2 * a0].reshape(B, S, N_ATTN_HEADS, HEAD_DIM).astype(jnp.float32)
        v = proj[..., 2 * a0:3 * a0].reshape(B, S, N_ATTN_HEADS, HEAD_DIM).astype(jnp.float32)
        u = proj[..., 3 * a0:3 * a0 + SGU_WIDTH]
        vs = proj[..., 3 * a0 + SGU_WIDTH:]
        q = apply_partial_rope(q, cos, sin)
        k = apply_partial_rope(k, cos, sin)
        attn_out = dilated_mixture_attention(q, k, v)
        sgu_out = spatial_gating(u, vs, sgu_ln_g[layer], sgu_ln_b[layer],
                                 w_spatial[layer], b_spatial[layer])
        mix = jnp.concatenate([attn_out, sgu_out], -1) @ w_out[layer].astype(jnp.float32)
        h = layer_norm(DEEPNORM_ALPHA * h + mix, ln1_g[layer], ln1_b[layer])
        moe = hierarchical_moe(h, w_group[layer], b_group[layer], w_expert[layer],
                               b_expert[layer], w_gate_up[layer], w_down[layer])
        h = layer_norm(DEEPNORM_ALPHA * h + moe, ln2_g[layer], ln2_b[layer])
    return h.astype(x.dtype)
```

```python
import functools

import jax
import jax.numpy as jnp
from jax import lax
from jax.experimental import pallas as pl
from jax.experimental.pallas import tpu as pltpu

D_MODEL = 1024
HEAD_DIM = 64
N_HEADS = 8
ATTN_WIDTH = N_HEADS * HEAD_DIM
N_SGU_GROUPS = 8
SGU_WIDTH = 512
IN_WIDTH = 3 * ATTN_WIDTH + 2 * SGU_WIDTH
DILATIONS = (1, 4, 16)
BLK = 128
ROPE_THETA = 500000.0
ROT_DIM = 16
N_GROUPS = 4
EXPERTS_PER_GROUP = 8
N_EXPERTS = N_GROUPS * EXPERTS_PER_GROUP
EXPERT_FF = 256
LN_EPS = 1e-5
NEG_INF = -1e30
DEPTH = 1
ALPHA = (2 * DEPTH) ** 0.25

LANES = 128
VMEM_LIMIT = 56 * 1024 * 1024

F32 = jnp.float32
BF16 = jnp.bfloat16


def _layer_norm(v, g, b):
    mu = jnp.mean(v, -1, keepdims=True)
    vc = v - mu
    var = jnp.mean(vc * vc, -1, keepdims=True)
    return vc * lax.rsqrt(var + LN_EPS) * g + b


def _gelu(v):
    return 0.5 * v * (1.0 + lax.erf(v * (2.0 ** -0.5)))


def _trig_kernel(pos_ref, inv_ref, cos_ref, sin_ref):
    ang = pos_ref[...] * inv_ref[...]
    cos_ref[...] = jnp.cos(ang)
    sin_ref[...] = jnp.sin(ang)


def _rope_tables(positions):
    T = positions.size
    half = ROT_DIM // 2
    rows = T * half // LANES
    pos = jnp.repeat(positions.reshape(-1).astype(F32), half).reshape(rows, LANES)
    inv = ROPE_THETA ** (-jnp.arange(0, ROT_DIM, 2, dtype=F32) / ROT_DIM)
    inv = jnp.tile(inv, LANES // half).reshape(1, LANES)
    cos, sin = pl.pallas_call(
        _trig_kernel,
        out_shape=[jax.ShapeDtypeStruct((rows, LANES), F32)] * 2,
        name="rope_trig",
    )(pos, inv)
    cos = cos.reshape(T, half)
    sin = sin.reshape(T, half)
    pad = HEAD_DIM - ROT_DIM
    one = jnp.ones((T, pad), F32)
    zero = jnp.zeros((T, pad), F32)
    z8 = jnp.zeros((T, half), F32)
    c64 = jnp.concatenate([cos, cos, one], -1)
    s1 = jnp.concatenate([-sin, z8, zero], -1)
    s2 = jnp.concatenate([z8, sin, zero], -1)
    rep = LANES // HEAD_DIM
    return jnp.tile(c64, (1, rep)), jnp.tile(s1, (1, rep)), jnp.tile(s2, (1, rep))


def _proj_kernel(x_ref, w_ref, c_ref, s1_ref, s2_ref, g_ref, b_ref, wsp_ref, bsp_ref,
                 q_ref, k_ref, v_ref, sgu_ref, *, tm):
    xb = x_ref[...].astype(BF16)
    c, s1, s2 = c_ref[...], s1_ref[...], s2_ref[...]

    def rope_store(out_ref, col0, scale):
        t = jnp.dot(xb, w_ref[:, col0:col0 + ATTN_WIDTH], preferred_element_type=F32)
        for j in range(ATTN_WIDTH // LANES):
            tj = t[:, j * LANES:(j + 1) * LANES]
            up = pltpu.roll(tj, LANES - ROT_DIM // 2, axis=1)
            dn = pltpu.roll(tj, ROT_DIM // 2, axis=1)
            r = tj * c + up * s1 + dn * s2
            if scale != 1.0:
                r = r * scale
            out_ref[:, j * LANES:(j + 1) * LANES] = r.astype(BF16)

    rope_store(q_ref, 0, HEAD_DIM ** -0.5)
    rope_store(k_ref, ATTN_WIDTH, 1.0)
    v_ref[...] = jnp.dot(xb, w_ref[:, 2 * ATTN_WIDTH:3 * ATTN_WIDTH],
                         preferred_element_type=F32).astype(BF16)

    u = _gelu(jnp.dot(xb, w_ref[:, 3 * ATTN_WIDTH:3 * ATTN_WIDTH + SGU_WIDTH],
                      preferred_element_type=F32))
    vs = _gelu(jnp.dot(xb, w_ref[:, 3 * ATTN_WIDTH + SGU_WIDTH:], preferred_element_type=F32))
    vs = _layer_norm(vs, g_ref[...], b_ref[...]).astype(BF16)

    row = lax.broadcasted_iota(jnp.int32, (2 * BLK, BLK), 0)
    col = lax.broadcasted_iota(jnp.int32, (2 * BLK, BLK), 1)
    causal = (row % BLK) >= col
    first_group = lax.broadcasted_iota(jnp.int32, (BLK, LANES), 1) < HEAD_DIM
    for p in range(SGU_WIDTH // LANES):
        wp = jnp.where(causal, wsp_ref[p], 0.0).astype(BF16)
        for ch in range(tm // BLK):
            vblk = vs[ch * BLK:(ch + 1) * BLK, p * LANES:(p + 1) * LANES]
            z2 = jnp.dot(wp, vblk, preferred_element_type=F32)
            z = jnp.where(first_group, z2[:BLK], z2[BLK:]) + bsp_ref[:, p * LANES:(p + 1) * LANES]
            ublk = u[ch * BLK:(ch + 1) * BLK, p * LANES:(p + 1) * LANES]
            sgu_ref[ch * BLK:(ch + 1) * BLK, p * LANES:(p + 1) * LANES] = (ublk * z).astype(BF16)


def _proj(x2d, w_in, tabs, sgu_g, sgu_b, w_sp, b_sp, tm=512):
    T = x2d.shape[0]
    row_blk = lambda w: pl.BlockSpec((tm, w), lambda i: (i, 0))
    full = lambda shape: pl.BlockSpec(shape, lambda i: (0,) * len(shape))
    out = jax.ShapeDtypeStruct((T, ATTN_WIDTH), BF16)
    return pl.pallas_call(
        functools.partial(_proj_kernel, tm=tm),
        grid=(T // tm,),
        in_specs=[row_blk(D_MODEL), full((D_MODEL, IN_WIDTH)),
                  row_blk(LANES), row_blk(LANES), row_blk(LANES),
                  full((1, SGU_WIDTH)), full((1, SGU_WIDTH)),
                  full((N_SGU_GROUPS // 2, 2 * BLK, BLK)), full((BLK, SGU_WIDTH))],
        out_specs=[row_blk(ATTN_WIDTH)] * 4,
        out_shape=[out] * 4,
        compiler_params=pltpu.CompilerParams(dimension_semantics=("parallel",),
                                             vmem_limit_bytes=VMEM_LIMIT),
        name="proj",
    )(x2d, w_in, *tabs, sgu_g, sgu_b, w_sp, b_sp)


_NT = (((1,), (1,)), ((), ()))


def _attn_kernel(q_ref, k_ref, v_ref, o_ref, qf, kf, vf, m_s, l_s, a_s, *, seq):
    qf[...] = q_ref[...].astype(F32)
    kf[...] = k_ref[...].astype(F32)
    vf[...] = v_ref[...].astype(F32)

    qi = lax.broadcasted_iota(jnp.int32, (BLK, BLK), 0)
    kj = lax.broadcasted_iota(jnp.int32, (BLK, BLK), 1)
    cur_bias = jnp.where(kj <= qi, 0.0, NEG_INF).astype(F32)
    prev_bias = jnp.where(kj >= qi, 0.0, NEG_INF).astype(F32)
    head0 = lax.broadcasted_iota(jnp.int32, (BLK, LANES), 1) < HEAD_DIM

    def band(qb, kc, vc, kp, vp):
        ms, ls, accs = [], [], []
        for h in range(2):
            qh = jnp.where(head0 if h == 0 else ~head0, qb, 0.0).astype(BF16)
            sc = lax.dot_general(qh, kc, _NT, preferred_element_type=F32) + cur_bias
            m = jnp.max(sc, -1, keepdims=True)
            if kp is not None:
                sp = lax.dot_general(qh, kp, _NT, preferred_element_type=F32) + prev_bias
                m = jnp.maximum(m, jnp.max(sp, -1, keepdims=True))
            pc = jnp.exp(sc - m)
            l = jnp.sum(pc, -1, keepdims=True)
            acc = jnp.dot(pc.astype(BF16), vc, preferred_element_type=F32)
            if kp is not None:
                pp = jnp.exp(sp - m)
                l = l + jnp.sum(pp, -1, keepdims=True)
                acc = acc + jnp.dot(pp.astype(BF16), vp, preferred_element_type=F32)
            ms.append(m)
            ls.append(l)
            accs.append(acc)
        return (jnp.where(head0, ms[0], ms[1]), jnp.where(head0, ls[0], ls[1]),
                jnp.where(head0, accs[0], accs[1]))

    for d in DILATIONS:
        sub_len = seq // d
        nblk = sub_len // BLK

        def rows(r, n):
            start = r + n * BLK * d
            if d == 1:
                return pl.ds(start, BLK)
            return pl.ds(start, BLK, stride=d)

        def residue(r, carry):
            kp = vp = None
            for n in range(nblk):
                idx = rows(r, n)
                qb = qf[idx, :]
                kc = kf[idx, :].astype(BF16)
                vc = vf[idx, :].astype(BF16)
                m, l, acc = band(qb, kc, vc, kp, vp)
                if d == DILATIONS[0]:
                    m_s[idx, :] = m
                    l_s[idx, :] = l
                    a_s[idx, :] = acc
                else:
                    mo, lo, ao = m_s[idx, :], l_s[idx, :], a_s[idx, :]
                    mn = jnp.maximum(mo, m)
                    wo = jnp.exp(mo - mn)
                    wn = jnp.exp(m - mn)
                    m_s[idx, :] = mn
                    l_s[idx, :] = wo * lo + wn * l
                    a_s[idx, :] = wo * ao + wn * acc
                kp, vp = kc, vc
            return carry

        if d == 1:
            residue(0, 0)
        else:
            lax.fori_loop(0, d, residue, 0)

    o_ref[...] = (a_s[...] / l_s[...]).astype(BF16)


def _attention(q, k, v, batch, seq):
    T = q.shape[0]
    blk = pl.BlockSpec((seq, LANES), lambda b, p: (b, p))
    scratch = [pltpu.VMEM((seq, LANES), F32)] * 6
    return pl.pallas_call(
        functools.partial(_attn_kernel, seq=seq),
        grid=(batch, ATTN_WIDTH // LANES),
        in_specs=[blk, blk, blk],
        out_specs=blk,
        out_shape=jax.ShapeDtypeStruct((T, ATTN_WIDTH), BF16),
        scratch_shapes=scratch,
        compiler_params=pltpu.CompilerParams(dimension_semantics=("parallel", "parallel"),
                                             vmem_limit_bytes=VMEM_LIMIT),
        name="attention",
    )(q, k, v)


def _outproj_kernel(attn_ref, sgu_ref, x_ref, wo_ref, g_ref, b_ref, wr_ref, br_ref,
                    h_ref, hb_ref, gate_ref):
    mix = jnp.dot(attn_ref[...], wo_ref[:ATTN_WIDTH, :], preferred_element_type=F32)
    mix = mix + jnp.dot(sgu_ref[...], wo_ref[ATTN_WIDTH:, :], preferred_element_type=F32)
    h = _layer_norm(ALPHA * x_ref[...] + mix, g_ref[...], b_ref[...])
    h_ref[...] = h
    hb_ref[...] = h.astype(BF16)

    logits = jnp.dot(h, wr_ref[...], preferred_element_type=F32,
                     precision=lax.Precision.HIGHEST) + br_ref[...]
    lane = lax.broadcasted_iota(jnp.int32, logits.shape, 1)
    is_group = (lane >= N_EXPERTS) & (lane < N_EXPERTS + N_GROUPS)
    gl = jnp.where(is_group, logits, NEG_INF)
    gmax = jnp.max(gl, -1, keepdims=True)
    p_sel = 1.0 / jnp.sum(jnp.where(is_group, jnp.exp(gl - gmax), 0.0), -1, keepdims=True)
    big = jnp.int32(2 ** 30)
    g_lane = jnp.min(jnp.where(is_group & (gl == gmax), lane, big), -1, keepdims=True)
    g_idx = g_lane - N_EXPERTS
    in_group = (lane >= g_idx * EXPERTS_PER_GROUP) & (lane < (g_idx + 1) * EXPERTS_PER_GROUP)
    e1 = jnp.where(in_group, logits, NEG_INF)
    v1 = jnp.max(e1, -1, keepdims=True)
    i1 = jnp.min(jnp.where(in_group & (e1 == v1), lane, big), -1, keepdims=True)
    rest = in_group & (lane != i1)
    e2 = jnp.where(rest, logits, NEG_INF)
    v2 = jnp.max(e2, -1, keepdims=True)
    i2 = jnp.min(jnp.where(rest & (e2 == v2), lane, big), -1, keepdims=True)
    t = jnp.exp(v2 - v1)
    w1 = 1.0 / (1.0 + t)
    w2 = t * w1
    gate_ref[...] = (jnp.where(lane == i1, w1, 0.0) + jnp.where(lane == i2, w2, 0.0)) * p_sel


def _outproj(attn, sgu, x2d, w_out, ln_g, ln_b, w_r, b_r, tm=512):
    T = x2d.shape[0]
    row_blk = lambda w: pl.BlockSpec((tm, w), lambda i: (i, 0))
    full = lambda shape: pl.BlockSpec(shape, lambda i: (0,) * len(shape))
    return pl.pallas_call(
        _outproj_kernel,
        grid=(T // tm,),
        in_specs=[row_blk(ATTN_WIDTH), row_blk(SGU_WIDTH), row_blk(D_MODEL),
                  full((D_MODEL, D_MODEL)), full((1, D_MODEL)), full((1, D_MODEL)),
                  full((D_MODEL, LANES)), full((1, LANES))],
        out_specs=[row_blk(D_MODEL), row_blk(D_MODEL), row_blk(LANES)],
        out_shape=[jax.ShapeDtypeStruct((T, D_MODEL), F32),
                   jax.ShapeDtypeStruct((T, D_MODEL), BF16),
                   jax.ShapeDtypeStruct((T, LANES), F32)],
        compiler_params=pltpu.CompilerParams(dimension_semantics=("parallel",),
                                             vmem_limit_bytes=VMEM_LIMIT),
        name="outproj",
    )(attn, sgu, x2d, w_out, ln_g, ln_b, w_r, b_r)


def _moe_kernel(hb_ref, h_ref, gate_ref, wgu_ref, wd_ref, g_ref, b_ref, o_ref, acc_ref):
    e = pl.program_id(1)

    @pl.when(e == 0)
    def _():
        acc_ref[...] = jnp.zeros_like(acc_ref)

    gates = gate_ref[...]
    lane = lax.broadcasted_iota(jnp.int32, gates.shape, 1)
    gcol = jnp.sum(jnp.where(lane == e, gates, 0.0), -1, keepdims=True)
    gu = jnp.dot(hb_ref[...], wgu_ref[0], preferred_element_type=F32)
    a, b = gu[:, :EXPERT_FF], gu[:, EXPERT_FF:]
    act = (a * jax.nn.sigmoid(a)) * b * gcol
    acc_ref[...] += jnp.dot(act.astype(BF16), wd_ref[0], preferred_element_type=F32)

    @pl.when(e == pl.num_programs(1) - 1)
    def _():
        o_ref[...] = _layer_norm(ALPHA * h_ref[...] + acc_ref[...], g_ref[...], b_ref[...])


def _moe(hb, h, gates, w_gu, w_d, ln_g, ln_b, tm=1024):
    T = h.shape[0]
    row_blk = lambda w: pl.BlockSpec((tm, w), lambda i, e: (i, 0))
    full = lambda shape: pl.BlockSpec(shape, lambda i, e: (0,) * len(shape))
    return pl.pallas_call(
        _moe_kernel,
        grid=(T // tm, N_EXPERTS),
        in_specs=[row_blk(D_MODEL), row_blk(D_MODEL), row_blk(LANES),
                  pl.BlockSpec((1, D_MODEL, 2 * EXPERT_FF), lambda i, e: (e, 0, 0)),
                  pl.BlockSpec((1, EXPERT_FF, D_MODEL), lambda i, e: (e, 0, 0)),
                  full((1, D_MODEL)), full((1, D_MODEL))],
        out_specs=row_blk(D_MODEL),
        out_shape=jax.ShapeDtypeStruct((T, D_MODEL), F32),
        scratch_shapes=[pltpu.VMEM((tm, D_MODEL), F32)],
        compiler_params=pltpu.CompilerParams(dimension_semantics=("parallel", "arbitrary"),
                                             vmem_limit_bytes=VMEM_LIMIT),
        name="moe",
    )(hb, h, gates, w_gu, w_d, ln_g, ln_b)


def kernel(x, positions, w_in, sgu_ln_g, sgu_ln_b, w_spatial, b_spatial, w_out, ln1_g, ln1_b,
           w_group, b_group, w_expert, b_expert, w_gate_up, w_down, ln2_g, ln2_b):
    B, S, D = x.shape
    T = B * S
    assert D == D_MODEL and S % (BLK * DILATIONS[-1]) == 0
    h = x.reshape(T, D).astype(F32)
    tabs = _rope_tables(positions)
    for layer in range(DEPTH):
        w_sp = w_spatial[layer].astype(F32).reshape(N_SGU_GROUPS // 2, 2 * BLK, BLK)
        b_sp = jnp.repeat(b_spatial[layer].astype(F32).T, HEAD_DIM, axis=1)
        q, k, v, sgu = _proj(h, w_in[layer].astype(BF16), tabs,
                             sgu_ln_g[layer].reshape(1, -1), sgu_ln_b[layer].reshape(1, -1),
                             w_sp, b_sp)
        attn = _attention(q, k, v, B, S)
        w_r = jnp.concatenate(
            [jnp.transpose(w_expert[layer], (1, 0, 2)).reshape(D, N_EXPERTS), w_group[layer],
             jnp.zeros((D, LANES - N_EXPERTS - N_GROUPS), F32)], -1).astype(F32)
        b_r = jnp.concatenate(
            [b_expert[layer].reshape(-1), b_group[layer],
             jnp.zeros((LANES - N_EXPERTS - N_GROUPS,), F32)]).reshape(1, LANES).astype(F32)
        h1, h1b, gates = _outproj(attn, sgu, h, w_out[layer].astype(BF16),
                                  ln1_g[layer].reshape(1, -1), ln1_b[layer].reshape(1, -1), w_r, b_r)
        w_gu = w_gate_up[layer].reshape(N_EXPERTS, D, 2 * EXPERT_FF).astype(BF16)
        w_d = w_down[layer].reshape(N_EXPERTS, EXPERT_FF, D).astype(BF16)
        h = _moe(h1b, h1, gates, w_gu, w_d, ln2_g[layer].reshape(1, -1), ln2_b[layer].reshape(1, -1))
    return h.reshape(B, S, D).astype(x.dtype)
```

```python
import functools

import jax
import jax.numpy as jnp
from jax import lax
from jax.experimental import pallas as pl
from jax.experimental.pallas import tpu as pltpu

D_MODEL = 1024
HEAD_DIM = 64
N_HEADS = 8
ATTN_WIDTH = N_HEADS * HEAD_DIM
N_SGU_GROUPS = 8
SGU_WIDTH = 512
IN_WIDTH = 3 * ATTN_WIDTH + 2 * SGU_WIDTH
DILATIONS = (1, 4, 16)
BLK = 128
GRP = 16
ROPE_THETA = 500000.0
ROT_DIM = 16
N_GROUPS = 4
EXPERTS_PER_GROUP = 8
N_EXPERTS = N_GROUPS * EXPERTS_PER_GROUP
EXPERT_FF = 256
LN_EPS = 1e-5
NEG_INF = -1e30
DEPTH = 1
ALPHA = (2 * DEPTH) ** 0.25

LANES = 128
VMEM_LIMIT = 56 * 1024 * 1024

F32 = jnp.float32
BF16 = jnp.bfloat16


def _layer_norm(v, g, b):
    mu = jnp.mean(v, -1, keepdims=True)
    vc = v - mu
    var = jnp.mean(vc * vc, -1, keepdims=True)
    return vc * lax.rsqrt(var + LN_EPS) * g + b


def _gelu(v):
    return 0.5 * v * (1.0 + lax.erf(v * (2.0 ** -0.5)))


def _trig_kernel(pos_ref, inv_ref, cos_ref, sin_ref):
    ang = pos_ref[...] * inv_ref[...]
    cos_ref[...] = jnp.cos(ang)
    sin_ref[...] = jnp.sin(ang)


def _rope_tables(positions):
    T = positions.size
    half = ROT_DIM // 2
    rows = T * half // LANES
    pos = jnp.repeat(positions.reshape(-1).astype(F32), half).reshape(rows, LANES)
    inv = ROPE_THETA ** (-jnp.arange(0, ROT_DIM, 2, dtype=F32) / ROT_DIM)
    inv = jnp.tile(inv, LANES // half).reshape(1, LANES)
    cos, sin = pl.pallas_call(
        _trig_kernel,
        out_shape=[jax.ShapeDtypeStruct((rows, LANES), F32)] * 2,
        name="rope_trig",
    )(pos, inv)
    cos = cos.reshape(T, half)
    sin = sin.reshape(T, half)
    pad = HEAD_DIM - ROT_DIM
    one = jnp.ones((T, pad), F32)
    zero = jnp.zeros((T, pad), F32)
    z8 = jnp.zeros((T, half), F32)
    c64 = jnp.concatenate([cos, cos, one], -1)
    s1 = jnp.concatenate([-sin, z8, zero], -1)
    s2 = jnp.concatenate([z8, sin, zero], -1)
    rep = LANES // HEAD_DIM
    return jnp.tile(c64, (1, rep)), jnp.tile(s1, (1, rep)), jnp.tile(s2, (1, rep))


def _proj_kernel(x_ref, w_ref, c_ref, s1_ref, s2_ref, g_ref, b_ref, wsp_ref, bsp_ref,
                 q_ref, k_ref, v_ref, sgu_ref, *, tm):
    xb = x_ref[...].astype(BF16)
    c, s1, s2 = c_ref[...], s1_ref[...], s2_ref[...]

    def rope_store(out_ref, col0, scale):
        t = jnp.dot(xb, w_ref[:, col0:col0 + ATTN_WIDTH], preferred_element_type=F32)
        for j in range(ATTN_WIDTH // LANES):
            tj = t[:, j * LANES:(j + 1) * LANES]
            up = pltpu.roll(tj, LANES - ROT_DIM // 2, axis=1)
            dn = pltpu.roll(tj, ROT_DIM // 2, axis=1)
            r = tj * c + up * s1 + dn * s2
            if scale != 1.0:
                r = r * scale
            out_ref[:, j * LANES:(j + 1) * LANES] = r.astype(BF16)

    rope_store(q_ref, 0, HEAD_DIM ** -0.5)
    rope_store(k_ref, ATTN_WIDTH, 1.0)
    v_ref[...] = jnp.dot(xb, w_ref[:, 2 * ATTN_WIDTH:3 * ATTN_WIDTH],
                         preferred_element_type=F32).astype(BF16)

    u = _gelu(jnp.dot(xb, w_ref[:, 3 * ATTN_WIDTH:3 * ATTN_WIDTH + SGU_WIDTH],
                      preferred_element_type=F32))
    vs = _gelu(jnp.dot(xb, w_ref[:, 3 * ATTN_WIDTH + SGU_WIDTH:], preferred_element_type=F32))
    vs = _layer_norm(vs, g_ref[...], b_ref[...]).astype(BF16)

    row = lax.broadcasted_iota(jnp.int32, (2 * BLK, BLK), 0)
    col = lax.broadcasted_iota(jnp.int32, (2 * BLK, BLK), 1)
    causal = (row % BLK) >= col
    first_group = lax.broadcasted_iota(jnp.int32, (BLK, LANES), 1) < HEAD_DIM
    for p in range(SGU_WIDTH // LANES):
        wp = jnp.where(causal, wsp_ref[p], 0.0).astype(BF16)
        for ch in range(tm // BLK):
            vblk = vs[ch * BLK:(ch + 1) * BLK, p * LANES:(p + 1) * LANES]
            z2 = jnp.dot(wp, vblk, preferred_element_type=F32)
            z = jnp.where(first_group, z2[:BLK], z2[BLK:]) + bsp_ref[:, p * LANES:(p + 1) * LANES]
            ublk = u[ch * BLK:(ch + 1) * BLK, p * LANES:(p + 1) * LANES]
            sgu_ref[ch * BLK:(ch + 1) * BLK, p * LANES:(p + 1) * LANES] = (ublk * z).astype(BF16)


def _proj(x2d, w_in, tabs, sgu_g, sgu_b, w_sp, b_sp, tm=512):
    T = x2d.shape[0]
    row_blk = lambda w: pl.BlockSpec((tm, w), lambda i: (i, 0))
    full = lambda shape: pl.BlockSpec(shape, lambda i: (0,) * len(shape))
    out = jax.ShapeDtypeStruct((T, ATTN_WIDTH), BF16)
    return pl.pallas_call(
        functools.partial(_proj_kernel, tm=tm),
        grid=(T // tm,),
        in_specs=[row_blk(D_MODEL), full((D_MODEL, IN_WIDTH)),
                  row_blk(LANES), row_blk(LANES), row_blk(LANES),
                  full((1, SGU_WIDTH)), full((1, SGU_WIDTH)),
                  full((N_SGU_GROUPS // 2, 2 * BLK, BLK)), full((BLK, SGU_WIDTH))],
        out_specs=[row_blk(ATTN_WIDTH)] * 4,
        out_shape=[out] * 4,
        compiler_params=pltpu.CompilerParams(dimension_semantics=("parallel",),
                                             vmem_limit_bytes=VMEM_LIMIT),
        name="proj",
    )(x2d, w_in, *tabs, sgu_g, sgu_b, w_sp, b_sp)


_NT = (((1,), (1,)), ((), ()))


def _attn_kernel(q_ref, k_ref, v_ref, o_ref, nat, t4, q4, k4, va1, va4, q16, k16, va16,
                 s_scr, p_scr, mb_scr, st_a, st_b, *, seq):
    nblk = seq // BLK
    n_grp = nblk // GRP
    quarter = seq // 4
    qi = lax.broadcasted_iota(jnp.int32, (2 * BLK, BLK), 0) % BLK
    kj = lax.broadcasted_iota(jnp.int32, (2 * BLK, BLK), 1)
    cur_bias = jnp.where(kj <= qi, 0.0, NEG_INF).astype(F32)
    prev_bias = jnp.where(kj >= qi, 0.0, NEG_INF).astype(F32)
    both_bias = jnp.concatenate([prev_bias, cur_bias], axis=1)
    head0 = lax.broadcasted_iota(jnp.int32, (BLK, LANES), 1) < HEAD_DIM

    ones = jnp.ones((seq, LANES), BF16)
    va1[:, LANES:] = ones
    va4[:, LANES:] = ones
    va16[:, LANES:] = ones
    va1[:, :LANES] = v_ref[...]
    nat[0] = q_ref[...].astype(F32)
    nat[1] = k_ref[...].astype(F32)
    nat[2] = v_ref[...].astype(F32)

    def split4(src, dst_f32, dsts):
        def body(r, carry):
            base = pl.multiple_of(r * quarter, quarter)
            for i in range(3):
                for n in range(quarter // BLK):
                    chunk = src[i, pl.ds(r + n * 4 * BLK, BLK, stride=4), :]
                    rows = pl.ds(base + n * BLK, BLK)
                    if dst_f32 is not None:
                        dst_f32[i, rows, :] = chunk
                    dsts[i][rows, :LANES] = chunk.astype(BF16)
            return carry
        lax.fori_loop(0, 4, body, 0)

    split4(nat, t4, (q4, k4, va4))
    split4(t4, None, (q16, k16, va16))

    def regroup(src, dst):
        def body(r, carry):
            base = pl.multiple_of(r * quarter, quarter)
            for i in range(3):
                for n in range(quarter // BLK):
                    dst[i, pl.ds(r + n * 4 * BLK, BLK, stride=4), :] = src[i, pl.ds(base + n * BLK, BLK), :]
            return carry
        lax.fori_loop(0, 4, body, 0)

    def group(b0, prev_flags, q_src, k_src, va_src, st, first):
        def rows(j, back=0, n=1):
            start = b0 * BLK + (j - back) * BLK
            if not isinstance(start, int):
                start = pl.multiple_of(start, BLK)
            return pl.ds(start, n * BLK)

        for j, has_prev in enumerate(prev_flags):
            qb = q_src[rows(j), :]
            zero = jnp.zeros_like(qb)
            q2 = jnp.concatenate([jnp.where(head0, qb, zero), jnp.where(head0, zero, qb)], axis=0)
            kk = k_src[rows(j, 1, 2), :] if has_prev else k_src[rows(j), :]
            s = lax.dot_general(q2, kk, _NT, preferred_element_type=F32)
            if has_prev:
                s_scr[j] = s
            else:
                s_scr[j, :, BLK:] = s
        for j, has_prev in enumerate(prev_flags):
            s = (s_scr[j] + both_bias) if has_prev else (s_scr[j, :, BLK:] + cur_bias)
            m = jnp.max(s, -1, keepdims=True)
            p = jnp.exp(s - m).astype(BF16)
            if has_prev:
                p_scr[j] = p
            else:
                p_scr[j, :, BLK:] = p
            mb_scr[j] = jnp.broadcast_to(m, (2 * BLK, LANES))
        for j, has_prev in enumerate(prev_flags):
            if has_prev:
                res = jnp.dot(p_scr[j], va_src[rows(j, 1, 2), :], preferred_element_type=F32)
            else:
                res = jnp.dot(p_scr[j, :, BLK:], va_src[rows(j), :], preferred_element_type=F32)
            acc = jnp.where(head0, res[:BLK, :LANES], res[BLK:, :LANES])
            l = jnp.where(head0, res[:BLK, LANES:], res[BLK:, LANES:])
            m = jnp.where(head0, mb_scr[j, :BLK, :], mb_scr[j, BLK:, :])
            idx = rows(j)
            if first:
                st[0, idx, :] = m
                st[1, idx, :] = l
                st[2, idx, :] = acc
            else:
                mo, lo, ao = st[0, idx, :], st[1, idx, :], st[2, idx, :]
                mn = jnp.maximum(mo, m)
                wo = jnp.exp(mo - mn)
                wn = jnp.exp(m - mn)
                st[0, idx, :] = mn
                st[1, idx, :] = wo * lo + wn * l
                st[2, idx, :] = wo * ao + wn * acc

    def pattern(seg, q_src, k_src, va_src, st, first):
        def flags(g):
            return tuple((g * GRP + j) % seg != 0 for j in range(GRP))
        start = 0
        while start < n_grp and flags(start) != flags(n_grp - 1):
            group(start * GRP, flags(start), q_src, k_src, va_src, st, first)
            start += 1

        def body(g, carry):
            group(g * GRP, flags(n_grp - 1), q_src, k_src, va_src, st, first)
            return carry
        lax.fori_loop(start, n_grp, body, 0)

    pattern(nblk // 16, q16, k16, va16, st_a, True)
    regroup(st_a, st_b)
    pattern(nblk // 4, q4, k4, va4, st_b, False)
    regroup(st_b, st_a)
    pattern(nblk, q_ref, k_ref, va1, st_a, False)
    o_ref[...] = (st_a[2] / st_a[1]).astype(BF16)


def _attention(q, k, v, batch, seq):
    T = q.shape[0]
    assert seq % (16 * BLK) == 0 and (seq // BLK) % GRP == 0 and DILATIONS == (1, 4, 16)
    blk = pl.BlockSpec((seq, LANES), lambda b, p: (b, p))
    half = pltpu.VMEM((seq, LANES), BF16)
    wide = pltpu.VMEM((seq, 2 * LANES), BF16)
    state = pltpu.VMEM((3, seq, LANES), F32)
    scratch = [state, state,
               half, half, wide, wide, half, half, wide,
               pltpu.VMEM((GRP, 2 * BLK, 2 * BLK), F32),
               pltpu.VMEM((GRP, 2 * BLK, 2 * BLK), BF16),
               pltpu.VMEM((GRP, 2 * BLK, LANES), F32),
               state, state]
    return pl.pallas_call(
        functools.partial(_attn_kernel, seq=seq),
        grid=(batch, ATTN_WIDTH // LANES),
        in_specs=[blk, blk, blk],
        out_specs=blk,
        out_shape=jax.ShapeDtypeStruct((T, ATTN_WIDTH), BF16),
        scratch_shapes=scratch,
        compiler_params=pltpu.CompilerParams(dimension_semantics=("parallel", "parallel"),
                                             vmem_limit_bytes=VMEM_LIMIT),
        name="attention",
    )(q, k, v)


def _outproj_kernel(attn_ref, sgu_ref, x_ref, wo_ref, g_ref, b_ref, wr_ref, br_ref,
                    h_ref, hb_ref, gate_ref):
    mix = jnp.dot(attn_ref[...], wo_ref[:ATTN_WIDTH, :], preferred_element_type=F32)
    mix = mix + jnp.dot(sgu_ref[...], wo_ref[ATTN_WIDTH:, :], preferred_element_type=F32)
    h = _layer_norm(ALPHA * x_ref[...] + mix, g_ref[...], b_ref[...])
    h_ref[...] = h
    hb_ref[...] = h.astype(BF16)

    logits = jnp.dot(h, wr_ref[...], preferred_element_type=F32,
                     precision=lax.Precision.HIGHEST) + br_ref[...]
    lane = lax.broadcasted_iota(jnp.int32, logits.shape, 1)
    is_group = (lane >= N_EXPERTS) & (lane < N_EXPERTS + N_GROUPS)
    gl = jnp.where(is_group, logits, NEG_INF)
    gmax = jnp.max(gl, -1, keepdims=True)
    p_sel = 1.0 / jnp.sum(jnp.where(is_group, jnp.exp(gl - gmax), 0.0), -1, keepdims=True)
    big = jnp.int32(2 ** 30)
    g_lane = jnp.min(jnp.where(is_group & (gl == gmax), lane, big), -1, keepdims=True)
    g_idx = g_lane - N_EXPERTS
    in_group = (lane >= g_idx * EXPERTS_PER_GROUP) & (lane < (g_idx + 1) * EXPERTS_PER_GROUP)
    e1 = jnp.where(in_group, logits, NEG_INF)
    v1 = jnp.max(e1, -1, keepdims=True)
    i1 = jnp.min(jnp.where(in_group & (e1 == v1), lane, big), -1, keepdims=True)
    rest = in_group & (lane != i1)
    e2 = jnp.where(rest, logits, NEG_INF)
    v2 = jnp.max(e2, -1, keepdims=True)
    i2 = jnp.min(jnp.where(rest & (e2 == v2), lane, big), -1, keepdims=True)
    t = jnp.exp(v2 - v1)
    w1 = 1.0 / (1.0 + t)
    w2 = t * w1
    gate_ref[...] = (jnp.where(lane == i1, w1, 0.0) + jnp.where(lane == i2, w2, 0.0)) * p_sel


def _outproj(attn, sgu, x2d, w_out, ln_g, ln_b, w_r, b_r, tm=512):
    T = x2d.shape[0]
    row_blk = lambda w: pl.BlockSpec((tm, w), lambda i: (i, 0))
    full = lambda shape: pl.BlockSpec(shape, lambda i: (0,) * len(shape))
    return pl.pallas_call(
        _outproj_kernel,
        grid=(T // tm,),
        in_specs=[row_blk(ATTN_WIDTH), row_blk(SGU_WIDTH), row_blk(D_MODEL),
                  full((D_MODEL, D_MODEL)), full((1, D_MODEL)), full((1, D_MODEL)),
                  full((D_MODEL, LANES)), full((1, LANES))],
        out_specs=[row_blk(D_MODEL), row_blk(D_MODEL), row_blk(LANES)],
        out_shape=[jax.ShapeDtypeStruct((T, D_MODEL), F32),
                   jax.ShapeDtypeStruct((T, D_MODEL), BF16),
                   jax.ShapeDtypeStruct((T, LANES), F32)],
        compiler_params=pltpu.CompilerParams(dimension_semantics=("parallel",),
                                             vmem_limit_bytes=VMEM_LIMIT),
        name="outproj",
    )(attn, sgu, x2d, w_out, ln_g, ln_b, w_r, b_r)


def _moe_kernel(hb_ref, h_ref, gate_ref, wgu_ref, wd_ref, g_ref, b_ref, o_ref, acc_ref):
    e = pl.program_id(1)

    @pl.when(e == 0)
    def _():
        acc_ref[...] = jnp.zeros_like(acc_ref)

    gates = gate_ref[...]
    lane = lax.broadcasted_iota(jnp.int32, gates.shape, 1)
    gcol = jnp.sum(jnp.where(lane == e, gates, 0.0), -1, keepdims=True)
    gu = jnp.dot(hb_ref[...], wgu_ref[0], preferred_element_type=F32)
    a, b = gu[:, :EXPERT_FF], gu[:, EXPERT_FF:]
    act = (a * jax.nn.sigmoid(a)) * b * gcol
    acc_ref[...] += jnp.dot(act.astype(BF16), wd_ref[0], preferred_element_type=F32)

    @pl.when(e == pl.num_programs(1) - 1)
    def _():
        o_ref[...] = _layer_norm(ALPHA * h_ref[...] + acc_ref[...], g_ref[...], b_ref[...])


def _moe(hb, h, gates, w_gu, w_d, ln_g, ln_b, tm=1024):
    T = h.shape[0]
    row_blk = lambda w: pl.BlockSpec((tm, w), lambda i, e: (i, 0))
    full = lambda shape: pl.BlockSpec(shape, lambda i, e: (0,) * len(shape))
    return pl.pallas_call(
        _moe_kernel,
        grid=(T // tm, N_EXPERTS),
        in_specs=[row_blk(D_MODEL), row_blk(D_MODEL), row_blk(LANES),
                  pl.BlockSpec((1, D_MODEL, 2 * EXPERT_FF), lambda i, e: (e, 0, 0)),
                  pl.BlockSpec((1, EXPERT_FF, D_MODEL), lambda i, e: (e, 0, 0)),
                  full((1, D_MODEL)), full((1, D_MODEL))],
        out_specs=row_blk(D_MODEL),
        out_shape=jax.ShapeDtypeStruct((T, D_MODEL), F32),
        scratch_shapes=[pltpu.VMEM((tm, D_MODEL), F32)],
        compiler_params=pltpu.CompilerParams(dimension_semantics=("parallel", "arbitrary"),
                                             vmem_limit_bytes=VMEM_LIMIT),
        name="moe",
    )(hb, h, gates, w_gu, w_d, ln_g, ln_b)


def kernel(x, positions, w_in, sgu_ln_g, sgu_ln_b, w_spatial, b_spatial, w_out, ln1_g, ln1_b,
           w_group, b_group, w_expert, b_expert, w_gate_up, w_down, ln2_g, ln2_b):
    B, S, D = x.shape
    T = B * S
    assert D == D_MODEL and S % (BLK * DILATIONS[-1]) == 0
    h = x.reshape(T, D).astype(F32)
    tabs = _rope_tables(positions)
    for layer in range(DEPTH):
        w_sp = w_spatial[layer].astype(F32).reshape(N_SGU_GROUPS // 2, 2 * BLK, BLK)
        b_sp = jnp.repeat(b_spatial[layer].astype(F32).T, HEAD_DIM, axis=1)
        q, k, v, sgu = _proj(h, w_in[layer].astype(BF16), tabs,
                             sgu_ln_g[layer].reshape(1, -1), sgu_ln_b[layer].reshape(1, -1),
                             w_sp, b_sp)
        attn = _attention(q, k, v, B, S)
        w_r = jnp.concatenate(
            [jnp.transpose(w_expert[layer], (1, 0, 2)).reshape(D, N_EXPERTS), w_group[layer],
             jnp.zeros((D, LANES - N_EXPERTS - N_GROUPS), F32)], -1).astype(F32)
        b_r = jnp.concatenate(
            [b_expert[layer].reshape(-1), b_group[layer],
             jnp.zeros((LANES - N_EXPERTS - N_GROUPS,), F32)]).reshape(1, LANES).astype(F32)
        h1, h1b, gates = _outproj(attn, sgu, h, w_out[layer].astype(BF16),
                                  ln1_g[layer].reshape(1, -1), ln1_b[layer].reshape(1, -1), w_r, b_r)
        w_gu = w_gate_up[layer].reshape(N_EXPERTS, D, 2 * EXPERT_FF).astype(BF16)
        w_d = w_down[layer].reshape(N_EXPERTS, EXPERT_FF, D).astype(BF16)
        h = _moe(h1b, h1, gates, w_gu, w_d, ln2_g[layer].reshape(1, -1), ln2_b[layer].reshape(1, -1))
    return h.reshape(B, S, D).astype(x.dtype)
```

```python
import functools

import jax
import jax.numpy as jnp
from jax import lax
from jax.experimental import pallas as pl
from jax.experimental.pallas import tpu as pltpu

D_MODEL = 1024
HEAD_DIM = 64
N_HEADS = 8
ATTN_WIDTH = N_HEADS * HEAD_DIM
N_SGU_GROUPS = 8
SGU_WIDTH = 512
IN_WIDTH = 3 * ATTN_WIDTH + 2 * SGU_WIDTH
DILATIONS = (1, 4, 16)
BLK = 128
GRP = 16
ROPE_THETA = 500000.0
ROT_DIM = 16
N_GROUPS = 4
EXPERTS_PER_GROUP = 8
N_EXPERTS = N_GROUPS * EXPERTS_PER_GROUP
EXPERT_FF = 256
TOP_K = 2
MOE_ROWS = 128
LN_EPS = 1e-5
NEG_INF = -1e30
DEPTH = 1
ALPHA = (2 * DEPTH) ** 0.25

LANES = 128
VMEM_LIMIT = 56 * 1024 * 1024

F32 = jnp.float32
BF16 = jnp.bfloat16


def _layer_norm(v, g, b):
    mu = jnp.mean(v, -1, keepdims=True)
    vc = v - mu
    var = jnp.mean(vc * vc, -1, keepdims=True)
    return vc * lax.rsqrt(var + LN_EPS) * g + b


def _gelu(v):
    return 0.5 * v * (1.0 + lax.erf(v * (2.0 ** -0.5)))


def _trig_kernel(pos_ref, inv_ref, cos_ref, sin_ref):
    ang = pos_ref[...] * inv_ref[...]
    cos_ref[...] = jnp.cos(ang)
    sin_ref[...] = jnp.sin(ang)


def _rope_tables(positions):
    T = positions.size
    half = ROT_DIM // 2
    rows = T * half // LANES
    pos = jnp.repeat(positions.reshape(-1).astype(F32), half).reshape(rows, LANES)
    inv = ROPE_THETA ** (-jnp.arange(0, ROT_DIM, 2, dtype=F32) / ROT_DIM)
    inv = jnp.tile(inv, LANES // half).reshape(1, LANES)
    cos, sin = pl.pallas_call(
        _trig_kernel,
        out_shape=[jax.ShapeDtypeStruct((rows, LANES), F32)] * 2,
        name="rope_trig",
    )(pos, inv)
    cos = cos.reshape(T, half)
    sin = sin.reshape(T, half)
    pad = HEAD_DIM - ROT_DIM
    one = jnp.ones((T, pad), F32)
    zero = jnp.zeros((T, pad), F32)
    z8 = jnp.zeros((T, half), F32)
    c64 = jnp.concatenate([cos, cos, one], -1)
    s1 = jnp.concatenate([-sin, z8, zero], -1)
    s2 = jnp.concatenate([z8, sin, zero], -1)
    rep = LANES // HEAD_DIM
    return jnp.tile(c64, (1, rep)), jnp.tile(s1, (1, rep)), jnp.tile(s2, (1, rep))


def _proj_kernel(x_ref, w_ref, c_ref, s1_ref, s2_ref, g_ref, b_ref, wsp_ref, bsp_ref,
                 q_ref, k_ref, v_ref, sgu_ref, *, tm):
    xb = x_ref[...].astype(BF16)
    c, s1, s2 = c_ref[...], s1_ref[...], s2_ref[...]

    def rope_store(out_ref, col0, scale):
        t = jnp.dot(xb, w_ref[:, col0:col0 + ATTN_WIDTH], preferred_element_type=F32)
        for j in range(ATTN_WIDTH // LANES):
            tj = t[:, j * LANES:(j + 1) * LANES]
            up = pltpu.roll(tj, LANES - ROT_DIM // 2, axis=1)
            dn = pltpu.roll(tj, ROT_DIM // 2, axis=1)
            r = tj * c + up * s1 + dn * s2
            if scale != 1.0:
                r = r * scale
            out_ref[:, j * LANES:(j + 1) * LANES] = r.astype(BF16)

    rope_store(q_ref, 0, HEAD_DIM ** -0.5)
    rope_store(k_ref, ATTN_WIDTH, 1.0)
    v_ref[...] = jnp.dot(xb, w_ref[:, 2 * ATTN_WIDTH:3 * ATTN_WIDTH],
                         preferred_element_type=F32).astype(BF16)

    u = _gelu(jnp.dot(xb, w_ref[:, 3 * ATTN_WIDTH:3 * ATTN_WIDTH + SGU_WIDTH],
                      preferred_element_type=F32))
    vs = _gelu(jnp.dot(xb, w_ref[:, 3 * ATTN_WIDTH + SGU_WIDTH:], preferred_element_type=F32))
    vs = _layer_norm(vs, g_ref[...], b_ref[...]).astype(BF16)

    row = lax.broadcasted_iota(jnp.int32, (2 * BLK, BLK), 0)
    col = lax.broadcasted_iota(jnp.int32, (2 * BLK, BLK), 1)
    causal = (row % BLK) >= col
    first_group = lax.broadcasted_iota(jnp.int32, (BLK, LANES), 1) < HEAD_DIM
    for p in range(SGU_WIDTH // LANES):
        wp = jnp.where(causal, wsp_ref[p], 0.0).astype(BF16)
        for ch in range(tm // BLK):
            vblk = vs[ch * BLK:(ch + 1) * BLK, p * LANES:(p + 1) * LANES]
            z2 = jnp.dot(wp, vblk, preferred_element_type=F32)
            z = jnp.where(first_group, z2[:BLK], z2[BLK:]) + bsp_ref[:, p * LANES:(p + 1) * LANES]
            ublk = u[ch * BLK:(ch + 1) * BLK, p * LANES:(p + 1) * LANES]
            sgu_ref[ch * BLK:(ch + 1) * BLK, p * LANES:(p + 1) * LANES] = (ublk * z).astype(BF16)


def _proj(x2d, w_in, tabs, sgu_g, sgu_b, w_sp, b_sp, tm=512):
    T = x2d.shape[0]
    row_blk = lambda w: pl.BlockSpec((tm, w), lambda i: (i, 0))
    full = lambda shape: pl.BlockSpec(shape, lambda i: (0,) * len(shape))
    out = jax.ShapeDtypeStruct((T, ATTN_WIDTH), BF16)
    return pl.pallas_call(
        functools.partial(_proj_kernel, tm=tm),
        grid=(T // tm,),
        in_specs=[row_blk(D_MODEL), full((D_MODEL, IN_WIDTH)),
                  row_blk(LANES), row_blk(LANES), row_blk(LANES),
                  full((1, SGU_WIDTH)), full((1, SGU_WIDTH)),
                  full((N_SGU_GROUPS // 2, 2 * BLK, BLK)), full((BLK, SGU_WIDTH))],
        out_specs=[row_blk(ATTN_WIDTH)] * 4,
        out_shape=[out] * 4,
        compiler_params=pltpu.CompilerParams(dimension_semantics=("parallel",),
                                             vmem_limit_bytes=VMEM_LIMIT),
        name="proj",
    )(x2d, w_in, *tabs, sgu_g, sgu_b, w_sp, b_sp)


_NT = (((1,), (1,)), ((), ()))


def _attn_kernel(q_ref, k_ref, v_ref, o_ref, nat, t4, q4, k4, va1, va4, q16, k16, va16,
                 s_scr, p_scr, mb_scr, st_a, st_b, *, seq):
    nblk = seq // BLK
    n_grp = nblk // GRP
    quarter = seq // 4
    qi = lax.broadcasted_iota(jnp.int32, (2 * BLK, BLK), 0) % BLK
    kj = lax.broadcasted_iota(jnp.int32, (2 * BLK, BLK), 1)
    cur_bias = jnp.where(kj <= qi, 0.0, NEG_INF).astype(F32)
    prev_bias = jnp.where(kj >= qi, 0.0, NEG_INF).astype(F32)
    both_bias = jnp.concatenate([prev_bias, cur_bias], axis=1)
    head0 = lax.broadcasted_iota(jnp.int32, (BLK, LANES), 1) < HEAD_DIM

    ones = jnp.ones((seq, LANES), BF16)
    va1[:, LANES:] = ones
    va4[:, LANES:] = ones
    va16[:, LANES:] = ones
    va1[:, :LANES] = v_ref[...]
    nat[0] = q_ref[...].astype(F32)
    nat[1] = k_ref[...].astype(F32)
    nat[2] = v_ref[...].astype(F32)

    def split4(src, dst_f32, dsts):
        def body(r, carry):
            base = pl.multiple_of(r * quarter, quarter)
            for i in range(3):
                for n in range(quarter // BLK):
                    chunk = src[i, pl.ds(r + n * 4 * BLK, BLK, stride=4), :]
                    rows = pl.ds(base + n * BLK, BLK)
                    if dst_f32 is not None:
                        dst_f32[i, rows, :] = chunk
                    dsts[i][rows, :LANES] = chunk.astype(BF16)
            return carry
        lax.fori_loop(0, 4, body, 0)

    split4(nat, t4, (q4, k4, va4))
    split4(t4, None, (q16, k16, va16))

    def regroup(src, dst):
        def body(r, carry):
            base = pl.multiple_of(r * quarter, quarter)
            for i in range(3):
                for n in range(quarter // BLK):
                    dst[i, pl.ds(r + n * 4 * BLK, BLK, stride=4), :] = src[i, pl.ds(base + n * BLK, BLK), :]
            return carry
        lax.fori_loop(0, 4, body, 0)

    def group(b0, prev_flags, q_src, k_src, va_src, st, first):
        def rows(j, back=0, n=1):
            start = b0 * BLK + (j - back) * BLK
            if not isinstance(start, int):
                start = pl.multiple_of(start, BLK)
            return pl.ds(start, n * BLK)

        for j, has_prev in enumerate(prev_flags):
            qb = q_src[rows(j), :]
            zero = jnp.zeros_like(qb)
            q2 = jnp.concatenate([jnp.where(head0, qb, zero), jnp.where(head0, zero, qb)], axis=0)
            kk = k_src[rows(j, 1, 2), :] if has_prev else k_src[rows(j), :]
            s = lax.dot_general(q2, kk, _NT, preferred_element_type=F32)
            if has_prev:
                s_scr[j] = s
            else:
                s_scr[j, :, BLK:] = s
        for j, has_prev in enumerate(prev_flags):
            s = (s_scr[j] + both_bias) if has_prev else (s_scr[j, :, BLK:] + cur_bias)
            m = jnp.max(s, -1, keepdims=True)
            p = jnp.exp(s - m).astype(BF16)
            if has_prev:
                p_scr[j] = p
            else:
                p_scr[j, :, BLK:] = p
            mb_scr[j] = jnp.broadcast_to(m, (2 * BLK, LANES))
        for j, has_prev in enumerate(prev_flags):
            if has_prev:
                res = jnp.dot(p_scr[j], va_src[rows(j, 1, 2), :], preferred_element_type=F32)
            else:
                res = jnp.dot(p_scr[j, :, BLK:], va_src[rows(j), :], preferred_element_type=F32)
            acc = jnp.where(head0, res[:BLK, :LANES], res[BLK:, :LANES])
            l = jnp.where(head0, res[:BLK, LANES:], res[BLK:, LANES:])
            m = jnp.where(head0, mb_scr[j, :BLK, :], mb_scr[j, BLK:, :])
            idx = rows(j)
            if first:
                st[0, idx, :] = m
                st[1, idx, :] = l
                st[2, idx, :] = acc
            else:
                mo, lo, ao = st[0, idx, :], st[1, idx, :], st[2, idx, :]
                mn = jnp.maximum(mo, m)
                wo = jnp.exp(mo - mn)
                wn = jnp.exp(m - mn)
                st[0, idx, :] = mn
                st[1, idx, :] = wo * lo + wn * l
                st[2, idx, :] = wo * ao + wn * acc

    def pattern(seg, q_src, k_src, va_src, st, first):
        def flags(g):
            return tuple((g * GRP + j) % seg != 0 for j in range(GRP))
        start = 0
        while start < n_grp and flags(start) != flags(n_grp - 1):
            group(start * GRP, flags(start), q_src, k_src, va_src, st, first)
            start += 1

        def body(g, carry):
            group(g * GRP, flags(n_grp - 1), q_src, k_src, va_src, st, first)
            return carry
        lax.fori_loop(start, n_grp, body, 0)

    pattern(nblk // 16, q16, k16, va16, st_a, True)
    regroup(st_a, st_b)
    pattern(nblk // 4, q4, k4, va4, st_b, False)
    regroup(st_b, st_a)
    pattern(nblk, q_ref, k_ref, va1, st_a, False)
    o_ref[...] = (st_a[2] / st_a[1]).astype(BF16)


def _attention(q, k, v, batch, seq):
    T = q.shape[0]
    assert seq % (16 * BLK) == 0 and (seq // BLK) % GRP == 0 and DILATIONS == (1, 4, 16)
    blk = pl.BlockSpec((seq, LANES), lambda b, p: (b, p))
    half = pltpu.VMEM((seq, LANES), BF16)
    wide = pltpu.VMEM((seq, 2 * LANES), BF16)
    state = pltpu.VMEM((3, seq, LANES), F32)
    scratch = [state, state,
               half, half, wide, wide, half, half, wide,
               pltpu.VMEM((GRP, 2 * BLK, 2 * BLK), F32),
               pltpu.VMEM((GRP, 2 * BLK, 2 * BLK), BF16),
               pltpu.VMEM((GRP, 2 * BLK, LANES), F32),
               state, state]
    return pl.pallas_call(
        functools.partial(_attn_kernel, seq=seq),
        grid=(batch, ATTN_WIDTH // LANES),
        in_specs=[blk, blk, blk],
        out_specs=blk,
        out_shape=jax.ShapeDtypeStruct((T, ATTN_WIDTH), BF16),
        scratch_shapes=scratch,
        compiler_params=pltpu.CompilerParams(dimension_semantics=("parallel", "parallel"),
                                             vmem_limit_bytes=VMEM_LIMIT),
        name="attention",
    )(q, k, v)


def _pack_bf16_pair(hi, lo):
    hi_bits = lax.bitcast_convert_type(hi.astype(BF16).astype(F32), jnp.uint32)
    lo_bits = lax.bitcast_convert_type(lo.astype(BF16).astype(F32), jnp.uint32)
    return hi_bits | (lo_bits >> 16)


def _unpack_bf16_pair(packed):
    hi = lax.bitcast_convert_type(packed & jnp.uint32(0xFFFF0000), F32)
    lo = lax.bitcast_convert_type(packed << 16, F32)
    return hi, lo


def _outproj_kernel(attn_ref, sgu_ref, x_ref, wo_ref, g_ref, b_ref, wr_ref, br_ref,
                    h_ref, hpk_ref, route_ref):
    mix = jnp.dot(attn_ref[...], wo_ref[:ATTN_WIDTH, :], preferred_element_type=F32)
    mix = mix + jnp.dot(sgu_ref[...], wo_ref[ATTN_WIDTH:, :], preferred_element_type=F32)
    h = _layer_norm(ALPHA * x_ref[...] + mix, g_ref[...], b_ref[...])
    h_ref[...] = h
    hpk_ref[...] = _pack_bf16_pair(h[:, :D_MODEL // 2], h[:, D_MODEL // 2:])

    logits = jnp.dot(h, wr_ref[...], preferred_element_type=F32,
                     precision=lax.Precision.HIGHEST) + br_ref[...]
    lane = lax.broadcasted_iota(jnp.int32, logits.shape, 1)
    is_group = (lane >= N_EXPERTS) & (lane < N_EXPERTS + N_GROUPS)
    gl = jnp.where(is_group, logits, NEG_INF)
    gmax = jnp.max(gl, -1, keepdims=True)
    p_sel = 1.0 / jnp.sum(jnp.where(is_group, jnp.exp(gl - gmax), 0.0), -1, keepdims=True)
    big = jnp.int32(2 ** 30)
    g_lane = jnp.min(jnp.where(is_group & (gl == gmax), lane, big), -1, keepdims=True)
    g_idx = g_lane - N_EXPERTS
    in_group = (lane >= g_idx * EXPERTS_PER_GROUP) & (lane < (g_idx + 1) * EXPERTS_PER_GROUP)
    e1 = jnp.where(in_group, logits, NEG_INF)
    v1 = jnp.max(e1, -1, keepdims=True)
    i1 = jnp.min(jnp.where(in_group & (e1 == v1), lane, big), -1, keepdims=True)
    rest = in_group & (lane != i1)
    e2 = jnp.where(rest, logits, NEG_INF)
    v2 = jnp.max(e2, -1, keepdims=True)
    i2 = jnp.min(jnp.where(rest & (e2 == v2), lane, big), -1, keepdims=True)
    t = jnp.exp(v2 - v1)
    w1 = 1.0 / (1.0 + t)
    w2 = t * w1
    route_ref[...] = jnp.where(lane == 0, i1.astype(F32),
                               jnp.where(lane == 1, i2.astype(F32),
                                         jnp.where(lane == 2, w1 * p_sel,
                                                   jnp.where(lane == 3, w2 * p_sel, 0.0))))


def _outproj(attn, sgu, x2d, w_out, ln_g, ln_b, w_r, b_r, tm=512):
    T = x2d.shape[0]
    row_blk = lambda w: pl.BlockSpec((tm, w), lambda i: (i, 0))
    full = lambda shape: pl.BlockSpec(shape, lambda i: (0,) * len(shape))
    return pl.pallas_call(
        _outproj_kernel,
        grid=(T // tm,),
        in_specs=[row_blk(ATTN_WIDTH), row_blk(SGU_WIDTH), row_blk(D_MODEL),
                  full((D_MODEL, D_MODEL)), full((1, D_MODEL)), full((1, D_MODEL)),
                  full((D_MODEL, LANES)), full((1, LANES))],
        out_specs=[row_blk(D_MODEL), row_blk(D_MODEL // 2), row_blk(LANES)],
        out_shape=[jax.ShapeDtypeStruct((T, D_MODEL), F32),
                   jax.ShapeDtypeStruct((T, D_MODEL // 2), jnp.uint32),
                   jax.ShapeDtypeStruct((T, LANES), F32)],
        compiler_params=pltpu.CompilerParams(dimension_semantics=("parallel",),
                                             vmem_limit_bytes=VMEM_LIMIT),
        name="outproj",
    )(attn, sgu, x2d, w_out, ln_g, ln_b, w_r, b_r)


def _route_plan(route, tb):
    nb = route.shape[0] // tb
    e = route[:, :TOP_K].astype(jnp.int32).reshape(nb, tb, TOP_K)
    w = route[:, TOP_K:2 * TOP_K].reshape(nb * tb * TOP_K)
    chosen = (e[..., None] == jnp.arange(N_EXPERTS, dtype=jnp.int32)).astype(jnp.int32).sum(2)
    rank = jnp.cumsum(chosen, axis=1) - chosen
    ntiles = (chosen.sum(1) + MOE_ROWS - 1) // MOE_ROWS
    off = (jnp.cumsum(ntiles, axis=1) - ntiles) * MOE_ROWS
    pos = jnp.take_along_axis(off[:, None, :] + rank, e, axis=2)
    return (pos.reshape(-1).astype(jnp.int32), w, off.reshape(-1).astype(jnp.int32),
            ntiles.reshape(-1).astype(jnp.int32))


def _moe_kernel(off_ref, nt_ref, pos_hbm, w_hbm, hpk_ref, h_ref, wgu_ref, wd_ref, g_ref, b_ref,
                o_ref, xbuf, pos_s, w_s, sem, *, tb):
    blk = pl.program_id(0)
    e = pl.program_id(1)
    half = D_MODEL // 2

    @pl.when(e == 0)
    def _dispatch():
        span = pl.ds(pl.multiple_of(blk * tb * TOP_K, tb * TOP_K), tb * TOP_K)
        cp = pltpu.make_async_copy(pos_hbm.at[span], pos_s, sem.at[0])
        cw = pltpu.make_async_copy(w_hbm.at[span], w_s, sem.at[1])
        cp.start()
        cw.start()
        xbuf[...] = jnp.zeros_like(xbuf)
        cp.wait()
        cw.wait()

        def body(i, carry):
            for j in range(8):
                t = i * 8 + j
                row = hpk_ref[pl.ds(t, 1), :]
                for k in range(TOP_K):
                    xbuf[pl.ds(pos_s[TOP_K * t + k], 1), :] = row
            return carry
        lax.fori_loop(0, tb // 8, body, 0)

    first = off_ref[blk * N_EXPERTS + e]

    def tile(n, carry):
        rows = pl.ds(pl.multiple_of(first + n * MOE_ROWS, MOE_ROWS), MOE_ROWS)
        x_hi, x_lo = _unpack_bf16_pair(xbuf[rows, :])
        gu = jnp.dot(x_hi.astype(BF16), wgu_ref[0, :half, :], preferred_element_type=F32)
        gu = gu + jnp.dot(x_lo.astype(BF16), wgu_ref[0, half:, :], preferred_element_type=F32)
        a, b = gu[:, :EXPERT_FF], gu[:, EXPERT_FF:]
        act = (a * jax.nn.sigmoid(a)) * b
        y = jnp.dot(act.astype(BF16), wd_ref[0], preferred_element_type=F32)
        xbuf[rows, :] = _pack_bf16_pair(y[:, :half], y[:, half:])
        return carry
    lax.fori_loop(0, nt_ref[blk * N_EXPERTS + e], tile, 0)

    @pl.when(e == pl.num_programs(1) - 1)
    def _combine():
        def body(i, carry):
            for j in range(8):
                t = i * 8 + j
                hi = lo = None
                for k in range(TOP_K):
                    gate = w_s[TOP_K * t + k]
                    y_hi, y_lo = _unpack_bf16_pair(xbuf[pl.ds(pos_s[TOP_K * t + k], 1), :])
                    hi = gate * y_hi if hi is None else hi + gate * y_hi
                    lo = gate * y_lo if lo is None else lo + gate * y_lo
                o_ref[pl.ds(t, 1), :half] = hi
                o_ref[pl.ds(t, 1), half:] = lo
            return carry
        lax.fori_loop(0, tb // 8, body, 0)
        o_ref[...] = _layer_norm(ALPHA * h_ref[...] + o_ref[...], g_ref[...], b_ref[...])


def _moe(hpk, h, route, w_gu, w_d, ln_g, ln_b, tb=2048):
    T = h.shape[0]
    tb = min(tb, T)
    assert T % tb == 0 and (tb * TOP_K) % 1024 == 0
    pos, gate, off, ntiles = _route_plan(route, tb)
    max_rows = tb * TOP_K + N_EXPERTS * MOE_ROWS
    once = pl.Buffered(1)
    row_blk = lambda w: pl.BlockSpec((tb, w), lambda i, e, *_: (i, 0), pipeline_mode=once)
    full = lambda shape: pl.BlockSpec(shape, lambda i, e, *_: (0,) * len(shape))
    grid_spec = pltpu.PrefetchScalarGridSpec(
        num_scalar_prefetch=2,
        grid=(T // tb, N_EXPERTS),
        in_specs=[pl.BlockSpec(memory_space=pl.ANY), pl.BlockSpec(memory_space=pl.ANY),
                  row_blk(D_MODEL // 2), row_blk(D_MODEL),
                  pl.BlockSpec((1, D_MODEL, 2 * EXPERT_FF), lambda i, e, *_: (e, 0, 0)),
                  pl.BlockSpec((1, EXPERT_FF, D_MODEL), lambda i, e, *_: (e, 0, 0)),
                  full((1, D_MODEL)), full((1, D_MODEL))],
        out_specs=pl.BlockSpec((tb, D_MODEL), lambda i, e, *_: (i, 0)),
        scratch_shapes=[pltpu.VMEM((max_rows, D_MODEL // 2), jnp.uint32),
                        pltpu.SMEM((tb * TOP_K,), jnp.int32),
                        pltpu.SMEM((tb * TOP_K,), F32),
                        pltpu.SemaphoreType.DMA((2,))])
    return pl.pallas_call(
        functools.partial(_moe_kernel, tb=tb),
        grid_spec=grid_spec,
        out_shape=jax.ShapeDtypeStruct((T, D_MODEL), F32),
        compiler_params=pltpu.CompilerParams(dimension_semantics=("parallel", "arbitrary"),
                                             vmem_limit_bytes=VMEM_LIMIT),
        name="moe",
    )(off, ntiles, pos, gate, hpk, h, w_gu, w_d, ln_g, ln_b)


def kernel(x, positions, w_in, sgu_ln_g, sgu_ln_b, w_spatial, b_spatial, w_out, ln1_g, ln1_b,
           w_group, b_group, w_expert, b_expert, w_gate_up, w_down, ln2_g, ln2_b):
    B, S, D = x.shape
    T = B * S
    assert D == D_MODEL and S % (BLK * DILATIONS[-1]) == 0
    h = x.reshape(T, D).astype(F32)
    tabs = _rope_tables(positions)
    for layer in range(DEPTH):
        w_sp = w_spatial[layer].astype(F32).reshape(N_SGU_GROUPS // 2, 2 * BLK, BLK)
        b_sp = jnp.repeat(b_spatial[layer].astype(F32).T, HEAD_DIM, axis=1)
        q, k, v, sgu = _proj(h, w_in[layer].astype(BF16), tabs,
                             sgu_ln_g[layer].reshape(1, -1), sgu_ln_b[layer].reshape(1, -1),
                             w_sp, b_sp)
        attn = _attention(q, k, v, B, S)
        w_r = jnp.concatenate(
            [jnp.transpose(w_expert[layer], (1, 0, 2)).reshape(D, N_EXPERTS), w_group[layer],
             jnp.zeros((D, LANES - N_EXPERTS - N_GROUPS), F32)], -1).astype(F32)
        b_r = jnp.concatenate(
            [b_expert[layer].reshape(-1), b_group[layer],
             jnp.zeros((LANES - N_EXPERTS - N_GROUPS,), F32)]).reshape(1, LANES).astype(F32)
        h1, h1pk, route = _outproj(attn, sgu, h, w_out[layer].astype(BF16),
                                   ln1_g[layer].reshape(1, -1), ln1_b[layer].reshape(1, -1), w_r, b_r)
        w_gu = w_gate_up[layer].reshape(N_EXPERTS, D, 2 * EXPERT_FF).astype(BF16)
        w_d = w_down[layer].reshape(N_EXPERTS, EXPERT_FF, D).astype(BF16)
        h = _moe(h1pk, h1, route, w_gu, w_d, ln2_g[layer].reshape(1, -1), ln2_b[layer].reshape(1, -1))
    return h.reshape(B, S, D).astype(x.dtype)
```

```python
import functools

import jax
import jax.numpy as jnp
from jax import lax
from jax.experimental import pallas as pl
from jax.experimental.pallas import tpu as pltpu

D_MODEL = 1024
HEAD_DIM = 64
N_HEADS = 8
ATTN_WIDTH = N_HEADS * HEAD_DIM
N_SGU_GROUPS = 8
SGU_WIDTH = 512
IN_WIDTH = 3 * ATTN_WIDTH + 2 * SGU_WIDTH
DILATIONS = (1, 4, 16)
BLK = 128
GRP = 16
ROPE_THETA = 500000.0
ROT_DIM = 16
N_GROUPS = 4
EXPERTS_PER_GROUP = 8
N_EXPERTS = N_GROUPS * EXPERTS_PER_GROUP
EXPERT_FF = 256
TOP_K = 2
MOE_PAD = 64
MOE_TILE = 256
LN_EPS = 1e-5
NEG_INF = -1e30
DEPTH = 1
ALPHA = (2 * DEPTH) ** 0.25

LANES = 128
VMEM_LIMIT = 56 * 1024 * 1024

F32 = jnp.float32
BF16 = jnp.bfloat16


def _layer_norm(v, g, b):
    mu = jnp.mean(v, -1, keepdims=True)
    vc = v - mu
    var = jnp.mean(vc * vc, -1, keepdims=True)
    return vc * lax.rsqrt(var + LN_EPS) * g + b


def _gelu(v):
    return 0.5 * v * (1.0 + lax.erf(v * (2.0 ** -0.5)))


def _trig_kernel(pos_ref, inv_ref, cos_ref, sin_ref):
    ang = pos_ref[...] * inv_ref[...]
    cos_ref[...] = jnp.cos(ang)
    sin_ref[...] = jnp.sin(ang)


def _rope_tables(positions):
    T = positions.size
    half = ROT_DIM // 2
    rows = T * half // LANES
    pos = jnp.repeat(positions.reshape(-1).astype(F32), half).reshape(rows, LANES)
    inv = ROPE_THETA ** (-jnp.arange(0, ROT_DIM, 2, dtype=F32) / ROT_DIM)
    inv = jnp.tile(inv, LANES // half).reshape(1, LANES)
    cos, sin = pl.pallas_call(
        _trig_kernel,
        out_shape=[jax.ShapeDtypeStruct((rows, LANES), F32)] * 2,
        name="rope_trig",
    )(pos, inv)
    cos = cos.reshape(T, half)
    sin = sin.reshape(T, half)
    pad = HEAD_DIM - ROT_DIM
    one = jnp.ones((T, pad), F32)
    zero = jnp.zeros((T, pad), F32)
    z8 = jnp.zeros((T, half), F32)
    c64 = jnp.concatenate([cos, cos, one], -1)
    s1 = jnp.concatenate([-sin, z8, zero], -1)
    s2 = jnp.concatenate([z8, sin, zero], -1)
    rep = LANES // HEAD_DIM
    return jnp.tile(c64, (1, rep)), jnp.tile(s1, (1, rep)), jnp.tile(s2, (1, rep))


def _proj_kernel(x_ref, w_ref, c_ref, s1_ref, s2_ref, g_ref, b_ref, wsp_ref, bsp_ref,
                 q_ref, k_ref, v_ref, sgu_ref, *, tm):
    xb = x_ref[...].astype(BF16)
    c, s1, s2 = c_ref[...], s1_ref[...], s2_ref[...]

    def rope_store(out_ref, col0, scale):
        t = jnp.dot(xb, w_ref[:, col0:col0 + ATTN_WIDTH], preferred_element_type=F32)
        for j in range(ATTN_WIDTH // LANES):
            tj = t[:, j * LANES:(j + 1) * LANES]
            up = pltpu.roll(tj, LANES - ROT_DIM // 2, axis=1)
            dn = pltpu.roll(tj, ROT_DIM // 2, axis=1)
            r = tj * c + up * s1 + dn * s2
            if scale != 1.0:
                r = r * scale
            out_ref[:, j * LANES:(j + 1) * LANES] = r.astype(BF16)

    rope_store(q_ref, 0, HEAD_DIM ** -0.5)
    rope_store(k_ref, ATTN_WIDTH, 1.0)
    v_ref[...] = jnp.dot(xb, w_ref[:, 2 * ATTN_WIDTH:3 * ATTN_WIDTH],
                         preferred_element_type=F32).astype(BF16)

    u = _gelu(jnp.dot(xb, w_ref[:, 3 * ATTN_WIDTH:3 * ATTN_WIDTH + SGU_WIDTH],
                      preferred_element_type=F32))
    vs = _gelu(jnp.dot(xb, w_ref[:, 3 * ATTN_WIDTH + SGU_WIDTH:], preferred_element_type=F32))
    vs = _layer_norm(vs, g_ref[...], b_ref[...]).astype(BF16)

    row = lax.broadcasted_iota(jnp.int32, (2 * BLK, BLK), 0)
    col = lax.broadcasted_iota(jnp.int32, (2 * BLK, BLK), 1)
    causal = (row % BLK) >= col
    first_group = lax.broadcasted_iota(jnp.int32, (BLK, LANES), 1) < HEAD_DIM
    for p in range(SGU_WIDTH // LANES):
        wp = jnp.where(causal, wsp_ref[p], 0.0).astype(BF16)
        for ch in range(tm // BLK):
            vblk = vs[ch * BLK:(ch + 1) * BLK, p * LANES:(p + 1) * LANES]
            z2 = jnp.dot(wp, vblk, preferred_element_type=F32)
            z = jnp.where(first_group, z2[:BLK], z2[BLK:]) + bsp_ref[:, p * LANES:(p + 1) * LANES]
            ublk = u[ch * BLK:(ch + 1) * BLK, p * LANES:(p + 1) * LANES]
            sgu_ref[ch * BLK:(ch + 1) * BLK, p * LANES:(p + 1) * LANES] = (ublk * z).astype(BF16)


def _proj(x2d, w_in, tabs, sgu_g, sgu_b, w_sp, b_sp, tm=512):
    T = x2d.shape[0]
    row_blk = lambda w: pl.BlockSpec((tm, w), lambda i: (i, 0))
    full = lambda shape: pl.BlockSpec(shape, lambda i: (0,) * len(shape))
    out = jax.ShapeDtypeStruct((T, ATTN_WIDTH), BF16)
    return pl.pallas_call(
        functools.partial(_proj_kernel, tm=tm),
        grid=(T // tm,),
        in_specs=[row_blk(D_MODEL), full((D_MODEL, IN_WIDTH)),
                  row_blk(LANES), row_blk(LANES), row_blk(LANES),
                  full((1, SGU_WIDTH)), full((1, SGU_WIDTH)),
                  full((N_SGU_GROUPS // 2, 2 * BLK, BLK)), full((BLK, SGU_WIDTH))],
        out_specs=[row_blk(ATTN_WIDTH)] * 4,
        out_shape=[out] * 4,
        compiler_params=pltpu.CompilerParams(dimension_semantics=("parallel",),
                                             vmem_limit_bytes=VMEM_LIMIT),
        name="proj",
    )(x2d, w_in, *tabs, sgu_g, sgu_b, w_sp, b_sp)


_NT = (((1,), (1,)), ((), ()))


def _attn_kernel(q_ref, k_ref, v_ref, o_ref, nat, t4, q4, k4, va1, va4, q16, k16, va16,
                 s_scr, p_scr, mb_scr, st_a, st_b, *, seq):
    nblk = seq // BLK
    n_grp = nblk // GRP
    quarter = seq // 4
    qi = lax.broadcasted_iota(jnp.int32, (2 * BLK, BLK), 0) % BLK
    kj = lax.broadcasted_iota(jnp.int32, (2 * BLK, BLK), 1)
    cur_bias = jnp.where(kj <= qi, 0.0, NEG_INF).astype(F32)
    prev_bias = jnp.where(kj >= qi, 0.0, NEG_INF).astype(F32)
    both_bias = jnp.concatenate([prev_bias, cur_bias], axis=1)
    head0 = lax.broadcasted_iota(jnp.int32, (BLK, LANES), 1) < HEAD_DIM

    ones = jnp.ones((seq, LANES), BF16)
    va1[:, LANES:] = ones
    va4[:, LANES:] = ones
    va16[:, LANES:] = ones
    va1[:, :LANES] = v_ref[...]
    nat[0] = q_ref[...].astype(F32)
    nat[1] = k_ref[...].astype(F32)
    nat[2] = v_ref[...].astype(F32)

    def split4(src, dst_f32, dsts):
        def body(r, carry):
            base = pl.multiple_of(r * quarter, quarter)
            for i in range(3):
                for n in range(quarter // BLK):
                    chunk = src[i, pl.ds(r + n * 4 * BLK, BLK, stride=4), :]
                    rows = pl.ds(base + n * BLK, BLK)
                    if dst_f32 is not None:
                        dst_f32[i, rows, :] = chunk
                    dsts[i][rows, :LANES] = chunk.astype(BF16)
            return carry
        lax.fori_loop(0, 4, body, 0)

    split4(nat, t4, (q4, k4, va4))
    split4(t4, None, (q16, k16, va16))

    def regroup(src, dst):
        def body(r, carry):
            base = pl.multiple_of(r * quarter, quarter)
            for i in range(3):
                for n in range(quarter // BLK):
                    dst[i, pl.ds(r + n * 4 * BLK, BLK, stride=4), :] = src[i, pl.ds(base + n * BLK, BLK), :]
            return carry
        lax.fori_loop(0, 4, body, 0)

    def group(b0, prev_flags, q_src, k_src, va_src, st, first):
        def rows(j, back=0, n=1):
            start = b0 * BLK + (j - back) * BLK
            if not isinstance(start, int):
                start = pl.multiple_of(start, BLK)
            return pl.ds(start, n * BLK)

        for j, has_prev in enumerate(prev_flags):
            qb = q_src[rows(j), :]
            zero = jnp.zeros_like(qb)
            q2 = jnp.concatenate([jnp.where(head0, qb, zero), jnp.where(head0, zero, qb)], axis=0)
            kk = k_src[rows(j, 1, 2), :] if has_prev else k_src[rows(j), :]
            s = lax.dot_general(q2, kk, _NT, preferred_element_type=F32)
            if has_prev:
                s_scr[j] = s
            else:
                s_scr[j, :, BLK:] = s
        for j, has_prev in enumerate(prev_flags):
            s = (s_scr[j] + both_bias) if has_prev else (s_scr[j, :, BLK:] + cur_bias)
            m = jnp.max(s, -1, keepdims=True)
            p = jnp.exp(s - m).astype(BF16)
            if has_prev:
                p_scr[j] = p
            else:
                p_scr[j, :, BLK:] = p
            mb_scr[j] = jnp.broadcast_to(m, (2 * BLK, LANES))
        for j, has_prev in enumerate(prev_flags):
            if has_prev:
                res = jnp.dot(p_scr[j], va_src[rows(j, 1, 2), :], preferred_element_type=F32)
            else:
                res = jnp.dot(p_scr[j, :, BLK:], va_src[rows(j), :], preferred_element_type=F32)
            acc = jnp.where(head0, res[:BLK, :LANES], res[BLK:, :LANES])
            l = jnp.where(head0, res[:BLK, LANES:], res[BLK:, LANES:])
            m = jnp.where(head0, mb_scr[j, :BLK, :], mb_scr[j, BLK:, :])
            idx = rows(j)
            if first:
                st[0, idx, :] = m
                st[1, idx, :] = l
                st[2, idx, :] = acc
            else:
                mo, lo, ao = st[0, idx, :], st[1, idx, :], st[2, idx, :]
                mn = jnp.maximum(mo, m)
                wo = jnp.exp(mo - mn)
                wn = jnp.exp(m - mn)
                st[0, idx, :] = mn
                st[1, idx, :] = wo * lo + wn * l
                st[2, idx, :] = wo * ao + wn * acc

    def pattern(seg, q_src, k_src, va_src, st, first):
        def flags(g):
            return tuple((g * GRP + j) % seg != 0 for j in range(GRP))
        start = 0
        while start < n_grp and flags(start) != flags(n_grp - 1):
            group(start * GRP, flags(start), q_src, k_src, va_src, st, first)
            start += 1

        def body(g, carry):
            group(g * GRP, flags(n_grp - 1), q_src, k_src, va_src, st, first)
            return carry
        lax.fori_loop(start, n_grp, body, 0)

    pattern(nblk // 16, q16, k16, va16, st_a, True)
    regroup(st_a, st_b)
    pattern(nblk // 4, q4, k4, va4, st_b, False)
    regroup(st_b, st_a)
    pattern(nblk, q_ref, k_ref, va1, st_a, False)
    o_ref[...] = (st_a[2] / st_a[1]).astype(BF16)


def _attention(q, k, v, batch, seq):
    T = q.shape[0]
    assert seq % (16 * BLK) == 0 and (seq // BLK) % GRP == 0 and DILATIONS == (1, 4, 16)
    blk = pl.BlockSpec((seq, LANES), lambda b, p: (b, p))
    half = pltpu.VMEM((seq, LANES), BF16)
    wide = pltpu.VMEM((seq, 2 * LANES), BF16)
    state = pltpu.VMEM((3, seq, LANES), F32)
    scratch = [state, state,
               half, half, wide, wide, half, half, wide,
               pltpu.VMEM((GRP, 2 * BLK, 2 * BLK), F32),
               pltpu.VMEM((GRP, 2 * BLK, 2 * BLK), BF16),
               pltpu.VMEM((GRP, 2 * BLK, LANES), F32),
               state, state]
    return pl.pallas_call(
        functools.partial(_attn_kernel, seq=seq),
        grid=(batch, ATTN_WIDTH // LANES),
        in_specs=[blk, blk, blk],
        out_specs=blk,
        out_shape=jax.ShapeDtypeStruct((T, ATTN_WIDTH), BF16),
        scratch_shapes=scratch,
        compiler_params=pltpu.CompilerParams(dimension_semantics=("parallel", "parallel"),
                                             vmem_limit_bytes=VMEM_LIMIT),
        name="attention",
    )(q, k, v)


def _outproj_kernel(attn_ref, sgu_ref, x_ref, wo_ref, g_ref, b_ref, wr_ref, br_ref,
                    h_ref, route_ref):
    mix = jnp.dot(attn_ref[...], wo_ref[:ATTN_WIDTH, :], preferred_element_type=F32)
    mix = mix + jnp.dot(sgu_ref[...], wo_ref[ATTN_WIDTH:, :], preferred_element_type=F32)
    h = _layer_norm(ALPHA * x_ref[...] + mix, g_ref[...], b_ref[...])
    h_ref[...] = h

    logits = jnp.dot(h, wr_ref[...], preferred_element_type=F32,
                     precision=lax.Precision.HIGHEST) + br_ref[...]
    lane = lax.broadcasted_iota(jnp.int32, logits.shape, 1)
    is_group = (lane >= N_EXPERTS) & (lane < N_EXPERTS + N_GROUPS)
    gl = jnp.where(is_group, logits, NEG_INF)
    gmax = jnp.max(gl, -1, keepdims=True)
    p_sel = 1.0 / jnp.sum(jnp.where(is_group, jnp.exp(gl - gmax), 0.0), -1, keepdims=True)
    big = jnp.int32(2 ** 30)
    g_lane = jnp.min(jnp.where(is_group & (gl == gmax), lane, big), -1, keepdims=True)
    g_idx = g_lane - N_EXPERTS
    in_group = (lane >= g_idx * EXPERTS_PER_GROUP) & (lane < (g_idx + 1) * EXPERTS_PER_GROUP)
    e1 = jnp.where(in_group, logits, NEG_INF)
    v1 = jnp.max(e1, -1, keepdims=True)
    i1 = jnp.min(jnp.where(in_group & (e1 == v1), lane, big), -1, keepdims=True)
    rest = in_group & (lane != i1)
    e2 = jnp.where(rest, logits, NEG_INF)
    v2 = jnp.max(e2, -1, keepdims=True)
    i2 = jnp.min(jnp.where(rest & (e2 == v2), lane, big), -1, keepdims=True)
    t = jnp.exp(v2 - v1)
    w1 = 1.0 / (1.0 + t)
    w2 = t * w1
    route_ref[...] = jnp.where(lane == 0, i1.astype(F32),
                               jnp.where(lane == 1, i2.astype(F32),
                                         jnp.where(lane == 2, w1 * p_sel,
                                                   jnp.where(lane == 3, w2 * p_sel, 0.0))))


def _outproj(attn, sgu, x2d, w_out, ln_g, ln_b, w_r, b_r, tm=512):
    T = x2d.shape[0]
    row_blk = lambda w: pl.BlockSpec((tm, w), lambda i: (i, 0))
    full = lambda shape: pl.BlockSpec(shape, lambda i: (0,) * len(shape))
    return pl.pallas_call(
        _outproj_kernel,
        grid=(T // tm,),
        in_specs=[row_blk(ATTN_WIDTH), row_blk(SGU_WIDTH), row_blk(D_MODEL),
                  full((D_MODEL, D_MODEL)), full((1, D_MODEL)), full((1, D_MODEL)),
                  full((D_MODEL, LANES)), full((1, LANES))],
        out_specs=[row_blk(D_MODEL), row_blk(LANES)],
        out_shape=[jax.ShapeDtypeStruct((T, D_MODEL), F32),
                   jax.ShapeDtypeStruct((T, LANES), F32)],
        compiler_params=pltpu.CompilerParams(dimension_semantics=("parallel",),
                                             vmem_limit_bytes=VMEM_LIMIT),
        name="outproj",
    )(attn, sgu, x2d, w_out, ln_g, ln_b, w_r, b_r)


def _route_pos_kernel(route_ref, posw_ref, meta_ref, rank_scr, *, tb):
    route = route_ref[...]
    lane = lax.broadcasted_iota(jnp.int32, route.shape, 1)
    sel = [lane == route[:, k:k + 1].astype(jnp.int32) for k in range(TOP_K)]
    chosen = jnp.where(sel[0] | sel[1], 1.0, 0.0).astype(BF16)
    chunk = 2 * LANES
    ri = lax.broadcasted_iota(jnp.int32, (chunk, chunk), 0)
    ci = lax.broadcasted_iota(jnp.int32, (chunk, chunk), 1)
    earlier = jnp.where(ri > ci, 1.0, 0.0).astype(BF16)
    count = jnp.zeros((1, LANES), F32)
    for ch in range(tb // chunk):
        c = chosen[ch * chunk:(ch + 1) * chunk, :]
        rank_scr[ch * chunk:(ch + 1) * chunk, :] = jnp.dot(earlier, c, preferred_element_type=F32) + count
        count = count + jnp.sum(c.astype(F32), axis=0, keepdims=True)
    padded = jnp.floor((count + (MOE_PAD - 1)) / MOE_PAD) * MOE_PAD
    li = lax.broadcasted_iota(jnp.int32, (LANES, LANES), 0)
    lj = lax.broadcasted_iota(jnp.int32, (LANES, LANES), 1)
    before = jnp.where(li < lj, 1.0, 0.0).astype(F32)
    first = jnp.dot(jnp.broadcast_to(padded, (8, LANES)), before, preferred_element_type=F32,
                    precision=lax.Precision.HIGHEST)[0:1, :]
    val = rank_scr[...] + first
    pos = [jnp.sum(jnp.where(s, val, 0.0), -1, keepdims=True) for s in sel]
    posw_ref[...] = jnp.where(lane == 0, pos[0],
                              jnp.where(lane == 1, pos[1],
                                        jnp.where((lane == 2) | (lane == 3), route, 0.0)))
    row = lax.broadcasted_iota(jnp.int32, (8, LANES), 0)
    meta_ref[...] = jnp.where(row == 0, first, jnp.where(row == 1, padded, 0.0))


def _route_pos(route, tb):
    T = route.shape[0]
    return pl.pallas_call(
        functools.partial(_route_pos_kernel, tb=tb),
        grid=(T // tb,),
        in_specs=[pl.BlockSpec((tb, LANES), lambda i: (i, 0))],
        out_specs=[pl.BlockSpec((tb, LANES), lambda i: (i, 0)), pl.BlockSpec((8, LANES), lambda i: (i, 0))],
        out_shape=[jax.ShapeDtypeStruct((T, LANES), F32),
                   jax.ShapeDtypeStruct((T // tb * 8, LANES), F32)],
        scratch_shapes=[pltpu.VMEM((tb, LANES), F32)],
        compiler_params=pltpu.CompilerParams(dimension_semantics=("parallel",)),
        name="route_pos",
    )(route)


def _moe_kernel(first_ref, rows_ref, pos_hbm, w_hbm, h_ref, wgu_ref, wd_ref, g_ref, b_ref,
                o_ref, xbuf, tmp, pos_s, w_s, sem, *, tb):
    blk = pl.program_id(0)
    e = pl.program_id(1)
    n_col = D_MODEL // LANES

    def token_rows(i):
        return pl.ds(pl.multiple_of(i * 8, 8), 8)

    @pl.when(e == 0)
    def _dispatch():
        span = pl.ds(pl.multiple_of(blk * tb * TOP_K, tb * TOP_K), tb * TOP_K)
        cp = pltpu.make_async_copy(pos_hbm.at[span], pos_s, sem.at[0])
        cw = pltpu.make_async_copy(w_hbm.at[span], w_s, sem.at[1])
        cp.start()
        cw.start()
        xbuf[...] = jnp.zeros_like(xbuf)
        cp.wait()
        cw.wait()

        def body(i, carry):
            slab = h_ref[token_rows(i), :]
            for c in range(n_col):
                tmp[c * 8:(c + 1) * 8, :] = slab[:, c * LANES:(c + 1) * LANES]
            for s in range(8):
                tile = tmp[pl.ds(s, n_col, stride=8), :]
                for k in range(TOP_K):
                    start = pl.multiple_of(pos_s[TOP_K * (i * 8 + s) + k], 8)
                    xbuf[pl.ds(start, 8), :] = tile
            return carry
        lax.fori_loop(0, tb // 8, body, 0)

    def expert_tile(row0, rows):
        base = pl.multiple_of(row0 * 8, 8 * MOE_PAD)
        cols = [pl.ds(base + c, rows, stride=8) for c in range(n_col)]
        x = jnp.concatenate([xbuf[cc, :] for cc in cols], axis=1).astype(BF16)
        gu = jnp.dot(x, wgu_ref[0], preferred_element_type=F32)
        a, b = gu[:, :EXPERT_FF], gu[:, EXPERT_FF:]
        act = (a * jax.nn.sigmoid(a)) * b
        y = jnp.dot(act.astype(BF16), wd_ref[0], preferred_element_type=F32)
        for c, cc in enumerate(cols):
            xbuf[cc, :] = y[:, c * LANES:(c + 1) * LANES]

    first = first_ref[blk * N_EXPERTS + e]
    n_rows = rows_ref[blk * N_EXPERTS + e]
    n_big = n_rows // MOE_TILE

    def big(n, carry):
        expert_tile(first + n * MOE_TILE, MOE_TILE)
        return carry
    lax.fori_loop(0, n_big, big, 0)
    rest = first + n_big * MOE_TILE
    size = MOE_TILE // 2
    while size >= MOE_PAD:
        @pl.when((n_rows & size) != 0)
        def _(rest=rest, size=size):
            expert_tile(rest, size)
        rest = rest + (n_rows & size)
        size //= 2

    @pl.when(e == pl.num_programs(1) - 1)
    def _combine():
        def body(i, carry):
            for s in range(8):
                t = i * 8 + s
                acc = None
                for k in range(TOP_K):
                    start = pl.multiple_of(pos_s[TOP_K * t + k], 8)
                    term = w_s[TOP_K * t + k] * xbuf[pl.ds(start, 8), :]
                    acc = term if acc is None else acc + term
                tmp[s * 8:(s + 1) * 8, :] = acc
            for c in range(n_col):
                o_ref[token_rows(i), c * LANES:(c + 1) * LANES] = tmp[pl.ds(c, 8, stride=8), :]
            return carry
        lax.fori_loop(0, tb // 8, body, 0)

        def norm(i, carry):
            rows = pl.ds(pl.multiple_of(i * MOE_TILE, MOE_TILE), MOE_TILE)
            o_ref[rows, :] = _layer_norm(ALPHA * h_ref[rows, :] + o_ref[rows, :], g_ref[...], b_ref[...])
            return carry
        lax.fori_loop(0, tb // MOE_TILE, norm, 0)


def _moe(h, route, w_gu, w_d, ln_g, ln_b, tb=2048):
    T = h.shape[0]
    tb = min(tb, T)
    nb = T // tb
    assert T % tb == 0 and (tb * TOP_K) % 1024 == 0 and tb % MOE_TILE == 0
    posw, meta = _route_pos(route, tb)
    pos = (posw[:, :TOP_K].astype(jnp.int32) * 8).reshape(-1)
    gate = posw[:, TOP_K:2 * TOP_K].reshape(-1)
    meta = meta.reshape(nb, 8, LANES)[:, :2, :N_EXPERTS].astype(jnp.int32)
    first, n_rows = meta[:, 0].reshape(-1), meta[:, 1].reshape(-1)
    max_rows = tb * TOP_K + N_EXPERTS * MOE_PAD
    once = pl.Buffered(1)
    row_blk = lambda w: pl.BlockSpec((tb, w), lambda i, e, *_: (i, 0), pipeline_mode=once)
    full = lambda shape: pl.BlockSpec(shape, lambda i, e, *_: (0,) * len(shape))
    grid_spec = pltpu.PrefetchScalarGridSpec(
        num_scalar_prefetch=2,
        grid=(nb, N_EXPERTS),
        in_specs=[pl.BlockSpec(memory_space=pl.ANY), pl.BlockSpec(memory_space=pl.ANY),
                  row_blk(D_MODEL),
                  pl.BlockSpec((1, D_MODEL, 2 * EXPERT_FF), lambda i, e, *_: (e, 0, 0)),
                  pl.BlockSpec((1, EXPERT_FF, D_MODEL), lambda i, e, *_: (e, 0, 0)),
                  full((1, D_MODEL)), full((1, D_MODEL))],
        out_specs=row_blk(D_MODEL),
        scratch_shapes=[pltpu.VMEM((max_rows * 8, LANES), F32),
                        pltpu.VMEM((8 * 8, LANES), F32),
                        pltpu.SMEM((tb * TOP_K,), jnp.int32),
                        pltpu.SMEM((tb * TOP_K,), F32),
                        pltpu.SemaphoreType.DMA((2,))])
    return pl.pallas_call(
        functools.partial(_moe_kernel, tb=tb),
        grid_spec=grid_spec,
        out_shape=jax.ShapeDtypeStruct((T, D_MODEL), F32),
        compiler_params=pltpu.CompilerParams(dimension_semantics=("parallel", "arbitrary"),
                                             vmem_limit_bytes=VMEM_LIMIT),
        name="moe",
    )(first, n_rows, pos, gate, h, w_gu, w_d, ln_g, ln_b)


def kernel(x, positions, w_in, sgu_ln_g, sgu_ln_b, w_spatial, b_spatial, w_out, ln1_g, ln1_b,
           w_group, b_group, w_expert, b_expert, w_gate_up, w_down, ln2_g, ln2_b):
    B, S, D = x.shape
    T = B * S
    assert D == D_MODEL and S % (BLK * DILATIONS[-1]) == 0
    h = x.reshape(T, D).astype(F32)
    tabs = _rope_tables(positions)
    for layer in range(DEPTH):
        w_sp = w_spatial[layer].astype(F32).reshape(N_SGU_GROUPS // 2, 2 * BLK, BLK)
        b_sp = jnp.repeat(b_spatial[layer].astype(F32).T, HEAD_DIM, axis=1)
        q, k, v, sgu = _proj(h, w_in[layer].astype(BF16), tabs,
                             sgu_ln_g[layer].reshape(1, -1), sgu_ln_b[layer].reshape(1, -1),
                             w_sp, b_sp)
        attn = _attention(q, k, v, B, S)
        w_r = jnp.concatenate(
            [jnp.transpose(w_expert[layer], (1, 0, 2)).reshape(D, N_EXPERTS), w_group[layer],
             jnp.zeros((D, LANES - N_EXPERTS - N_GROUPS), F32)], -1).astype(F32)
        b_r = jnp.concatenate(
            [b_expert[layer].reshape(-1), b_group[layer],
             jnp.zeros((LANES - N_EXPERTS - N_GROUPS,), F32)]).reshape(1, LANES).astype(F32)
        h1, route = _outproj(attn, sgu, h, w_out[layer].astype(BF16),
                                   ln1_g[layer].reshape(1, -1), ln1_b[layer].reshape(1, -1), w_r, b_r)
        w_gu = w_gate_up[layer].reshape(N_EXPERTS, D, 2 * EXPERT_FF).astype(BF16)
        w_d = w_down[layer].reshape(N_EXPERTS, EXPERT_FF, D).astype(BF16)
        h = _moe(h1, route, w_gu, w_d, ln2_g[layer].reshape(1, -1), ln2_b[layer].reshape(1, -1))
    return h.reshape(B, S, D).astype(x.dtype)
```

```python
import functools

import jax
import jax.numpy as jnp
from jax import lax
from jax.experimental import pallas as pl
from jax.experimental.pallas import tpu as pltpu

D_MODEL = 1024
HEAD_DIM = 64
N_HEADS = 8
ATTN_WIDTH = N_HEADS * HEAD_DIM
N_SGU_GROUPS = 8
SGU_WIDTH = 512
IN_WIDTH = 3 * ATTN_WIDTH + 2 * SGU_WIDTH
DILATIONS = (1, 4, 16)
BLK = 128
GRP = 16
ROPE_THETA = 500000.0
ROT_DIM = 16
ROT_HALF = ROT_DIM // 2
N_GROUPS = 4
EXPERTS_PER_GROUP = 8
N_EXPERTS = N_GROUPS * EXPERTS_PER_GROUP
EXPERT_FF = 256
TOP_K = 2
MOE_PAD = 64
MOE_TILE = 256
LN_EPS = 1e-5
NEG_INF = -1e30
DEPTH = 1
ALPHA = (2 * DEPTH) ** 0.25

LANES = 128
VMEM_LIMIT = 56 * 1024 * 1024

F32 = jnp.float32
BF16 = jnp.bfloat16


def _layer_norm(v, g, b):
    mu = jnp.mean(v, -1, keepdims=True)
    vc = v - mu
    var = jnp.mean(vc * vc, -1, keepdims=True)
    return vc * lax.rsqrt(var + LN_EPS) * g + b


def _gelu(v):
    return 0.5 * v * (1.0 + lax.erf(v * (2.0 ** -0.5)))


def _trig_kernel(pos_ref, inv_ref, cos_ref, sin_ref):
    ang = pos_ref[...] * inv_ref[...]
    cos_ref[...] = jnp.cos(ang)
    sin_ref[...] = jnp.sin(ang)


def _rope_trig(positions):
    T = positions.size
    rows = T * ROT_HALF // LANES
    pos = jnp.repeat(positions.reshape(-1).astype(F32), ROT_HALF).reshape(rows, LANES)
    inv = ROPE_THETA ** (-jnp.arange(0, ROT_DIM, 2, dtype=F32) / ROT_DIM)
    inv = jnp.tile(inv, LANES // ROT_HALF).reshape(1, LANES)
    return pl.pallas_call(
        _trig_kernel,
        out_shape=[jax.ShapeDtypeStruct((rows, LANES), F32)] * 2,
        name="rope_trig",
    )(pos, inv)


def _proj_kernel(x_ref, w_ref, cos_ref, sin_ref, g_ref, b_ref, wsp_ref, bsp_ref,
                 q_ref, k_ref, v_ref, sgu_ref, c_tab, s1_tab, s2_tab, *, tm):
    xb = x_ref[...].astype(BF16)

    per_row = LANES // ROT_HALF
    lane = lax.broadcasted_iota(jnp.int32, (tm // per_row, LANES), 1)
    rotary = lane % HEAD_DIM < ROT_DIM
    for u in range(per_row):
        cs, sn = cos_ref[...], sin_ref[...]
        if u:
            cs = pltpu.roll(cs, LANES - ROT_HALF * u, axis=1)
            sn = pltpu.roll(sn, LANES - ROT_HALF * u, axis=1)
        cs = jnp.where(lane < ROT_HALF, cs, pltpu.roll(cs, ROT_HALF, axis=1))
        cs = jnp.where(rotary, jnp.where(lane < HEAD_DIM, cs, pltpu.roll(cs, HEAD_DIM, axis=1)), 1.0)
        sn = jnp.where(lane < ROT_HALF, sn, 0.0)
        neg = -(sn + pltpu.roll(sn, HEAD_DIM, axis=1))
        rows = pl.ds(u, tm // per_row, stride=per_row)
        c_tab[rows, :] = cs
        s1_tab[rows, :] = neg
        s2_tab[rows, :] = -pltpu.roll(neg, ROT_HALF, axis=1)
    c, s1, s2 = c_tab[...], s1_tab[...], s2_tab[...]

    def rope_store(out_ref, col0, scale):
        t = jnp.dot(xb, w_ref[:, col0:col0 + ATTN_WIDTH], preferred_element_type=F32)
        for j in range(ATTN_WIDTH // LANES):
            tj = t[:, j * LANES:(j + 1) * LANES]
            up = pltpu.roll(tj, LANES - ROT_HALF, axis=1)
            dn = pltpu.roll(tj, ROT_HALF, axis=1)
            r = tj * c + up * s1 + dn * s2
            if scale != 1.0:
                r = r * scale
            out_ref[:, j * LANES:(j + 1) * LANES] = r.astype(BF16)

    rope_store(q_ref, 0, HEAD_DIM ** -0.5)
    rope_store(k_ref, ATTN_WIDTH, 1.0)
    v_ref[...] = jnp.dot(xb, w_ref[:, 2 * ATTN_WIDTH:3 * ATTN_WIDTH],
                         preferred_element_type=F32).astype(BF16)

    u = _gelu(jnp.dot(xb, w_ref[:, 3 * ATTN_WIDTH:3 * ATTN_WIDTH + SGU_WIDTH],
                      preferred_element_type=F32))
    vs = _gelu(jnp.dot(xb, w_ref[:, 3 * ATTN_WIDTH + SGU_WIDTH:], preferred_element_type=F32))
    vs = _layer_norm(vs, g_ref[...], b_ref[...]).astype(BF16)

    row = lax.broadcasted_iota(jnp.int32, (2 * BLK, BLK), 0)
    col = lax.broadcasted_iota(jnp.int32, (2 * BLK, BLK), 1)
    causal = (row % BLK) >= col
    first_group = lax.broadcasted_iota(jnp.int32, (BLK, LANES), 1) < HEAD_DIM
    for p in range(SGU_WIDTH // LANES):
        wp = jnp.where(causal, wsp_ref[p], 0.0).astype(BF16)
        for ch in range(tm // BLK):
            vblk = vs[ch * BLK:(ch + 1) * BLK, p * LANES:(p + 1) * LANES]
            z2 = jnp.dot(wp, vblk, preferred_element_type=F32)
            z = jnp.where(first_group, z2[:BLK], z2[BLK:]) + bsp_ref[:, p * LANES:(p + 1) * LANES]
            ublk = u[ch * BLK:(ch + 1) * BLK, p * LANES:(p + 1) * LANES]
            sgu_ref[ch * BLK:(ch + 1) * BLK, p * LANES:(p + 1) * LANES] = (ublk * z).astype(BF16)


def _proj(x2d, w_in, trig, sgu_g, sgu_b, w_sp, b_sp, tm=512):
    T = x2d.shape[0]
    row_blk = lambda w: pl.BlockSpec((tm, w), lambda i: (i, 0))
    full = lambda shape: pl.BlockSpec(shape, lambda i: (0,) * len(shape))
    trig_blk = pl.BlockSpec((tm * ROT_HALF // LANES, LANES), lambda i: (i, 0))
    out = jax.ShapeDtypeStruct((T, ATTN_WIDTH), BF16)
    return pl.pallas_call(
        functools.partial(_proj_kernel, tm=tm),
        grid=(T // tm,),
        in_specs=[row_blk(D_MODEL), full((D_MODEL, IN_WIDTH)), trig_blk, trig_blk,
                  full((1, SGU_WIDTH)), full((1, SGU_WIDTH)),
                  full((N_SGU_GROUPS // 2, 2 * BLK, BLK)), full((BLK, SGU_WIDTH))],
        out_specs=[row_blk(ATTN_WIDTH)] * 4,
        out_shape=[out] * 4,
        scratch_shapes=[pltpu.VMEM((tm, LANES), F32)] * 3,
        compiler_params=pltpu.CompilerParams(dimension_semantics=("parallel",),
                                             vmem_limit_bytes=VMEM_LIMIT),
        name="proj",
    )(x2d, w_in, *trig, sgu_g, sgu_b, w_sp, b_sp)


_NT = (((1,), (1,)), ((), ()))


def _attn_kernel(q_ref, k_ref, v_ref, o_ref, nat, t4, q4, k4, va1, va4, q16, k16, va16,
                 s_scr, p_scr, mb_scr, st_a, st_b, *, seq):
    nblk = seq // BLK
    n_grp = nblk // GRP
    quarter = seq // 4
    qi = lax.broadcasted_iota(jnp.int32, (2 * BLK, BLK), 0) % BLK
    kj = lax.broadcasted_iota(jnp.int32, (2 * BLK, BLK), 1)
    cur_bias = jnp.where(kj <= qi, 0.0, NEG_INF).astype(F32)
    prev_bias = jnp.where(kj >= qi, 0.0, NEG_INF).astype(F32)
    both_bias = jnp.concatenate([prev_bias, cur_bias], axis=1)
    head0 = lax.broadcasted_iota(jnp.int32, (BLK, LANES), 1) < HEAD_DIM

    ones = jnp.ones((seq, LANES), BF16)
    va1[:, LANES:] = ones
    va4[:, LANES:] = ones
    va16[:, LANES:] = ones
    va1[:, :LANES] = v_ref[...]
    nat[0] = q_ref[...].astype(F32)
    nat[1] = k_ref[...].astype(F32)
    nat[2] = v_ref[...].astype(F32)

    def split4(src, dst_f32, dsts):
        def body(r, carry):
            base = pl.multiple_of(r * quarter, quarter)
            for i in range(3):
                for n in range(quarter // BLK):
                    chunk = src[i, pl.ds(r + n * 4 * BLK, BLK, stride=4), :]
                    rows = pl.ds(base + n * BLK, BLK)
                    if dst_f32 is not None:
                        dst_f32[i, rows, :] = chunk
                    dsts[i][rows, :LANES] = chunk.astype(BF16)
            return carry
        lax.fori_loop(0, 4, body, 0)

    split4(nat, t4, (q4, k4, va4))
    split4(t4, None, (q16, k16, va16))

    def regroup(src, dst):
        def body(r, carry):
            base = pl.multiple_of(r * quarter, quarter)
            for i in range(3):
                for n in range(quarter // BLK):
                    dst[i, pl.ds(r + n * 4 * BLK, BLK, stride=4), :] = src[i, pl.ds(base + n * BLK, BLK), :]
            return carry
        lax.fori_loop(0, 4, body, 0)

    def group(b0, prev_flags, q_src, k_src, va_src, st, first):
        def rows(j, back=0, n=1):
            start = b0 * BLK + (j - back) * BLK
            if not isinstance(start, int):
                start = pl.multiple_of(start, BLK)
            return pl.ds(start, n * BLK)

        for j, has_prev in enumerate(prev_flags):
            qb = q_src[rows(j), :]
            zero = jnp.zeros_like(qb)
            q2 = jnp.concatenate([jnp.where(head0, qb, zero), jnp.where(head0, zero, qb)], axis=0)
            kk = k_src[rows(j, 1, 2), :] if has_prev else k_src[rows(j), :]
            s = lax.dot_general(q2, kk, _NT, preferred_element_type=F32)
            if has_prev:
                s_scr[j] = s
            else:
                s_scr[j, :, BLK:] = s
        for j, has_prev in enumerate(prev_flags):
            s = (s_scr[j] + both_bias) if has_prev else (s_scr[j, :, BLK:] + cur_bias)
            m = jnp.max(s, -1, keepdims=True)
            p = jnp.exp(s - m).astype(BF16)
            if has_prev:
                p_scr[j] = p
            else:
                p_scr[j, :, BLK:] = p
            mb_scr[j] = jnp.broadcast_to(m, (2 * BLK, LANES))
        for j, has_prev in enumerate(prev_flags):
            if has_prev:
                res = jnp.dot(p_scr[j], va_src[rows(j, 1, 2), :], preferred_element_type=F32)
            else:
                res = jnp.dot(p_scr[j, :, BLK:], va_src[rows(j), :], preferred_element_type=F32)
            acc = jnp.where(head0, res[:BLK, :LANES], res[BLK:, :LANES])
            l = jnp.where(head0, res[:BLK, LANES:], res[BLK:, LANES:])
            m = jnp.where(head0, mb_scr[j, :BLK, :], mb_scr[j, BLK:, :])
            idx = rows(j)
            if first:
                st[0, idx, :] = m
                st[1, idx, :] = l
                st[2, idx, :] = acc
            else:
                mo, lo, ao = st[0, idx, :], st[1, idx, :], st[2, idx, :]
                mn = jnp.maximum(mo, m)
                wo = jnp.exp(mo - mn)
                wn = jnp.exp(m - mn)
                st[0, idx, :] = mn
                st[1, idx, :] = wo * lo + wn * l
                st[2, idx, :] = wo * ao + wn * acc

    def pattern(seg, q_src, k_src, va_src, st, first):
        def flags(g):
            return tuple((g * GRP + j) % seg != 0 for j in range(GRP))
        start = 0
        while start < n_grp and flags(start) != flags(n_grp - 1):
            group(start * GRP, flags(start), q_src, k_src, va_src, st, first)
            start += 1

        def body(g, carry):
            group(g * GRP, flags(n_grp - 1), q_src, k_src, va_src, st, first)
            return carry
        lax.fori_loop(start, n_grp, body, 0)

    pattern(nblk // 16, q16, k16, va16, st_a, True)
    regroup(st_a, st_b)
    pattern(nblk // 4, q4, k4, va4, st_b, False)
    regroup(st_b, st_a)
    pattern(nblk, q_ref, k_ref, va1, st_a, False)
    o_ref[...] = (st_a[2] / st_a[1]).astype(BF16)


def _attention(q, k, v, batch, seq):
    T = q.shape[0]
    assert seq % (16 * BLK) == 0 and (seq // BLK) % GRP == 0 and DILATIONS == (1, 4, 16)
    blk = pl.BlockSpec((seq, LANES), lambda b, p: (b, p))
    half = pltpu.VMEM((seq, LANES), BF16)
    wide = pltpu.VMEM((seq, 2 * LANES), BF16)
    state = pltpu.VMEM((3, seq, LANES), F32)
    scratch = [state, state,
               half, half, wide, wide, half, half, wide,
               pltpu.VMEM((GRP, 2 * BLK, 2 * BLK), F32),
               pltpu.VMEM((GRP, 2 * BLK, 2 * BLK), BF16),
               pltpu.VMEM((GRP, 2 * BLK, LANES), F32),
               state, state]
    return pl.pallas_call(
        functools.partial(_attn_kernel, seq=seq),
        grid=(batch, ATTN_WIDTH // LANES),
        in_specs=[blk, blk, blk],
        out_specs=blk,
        out_shape=jax.ShapeDtypeStruct((T, ATTN_WIDTH), BF16),
        scratch_shapes=scratch,
        compiler_params=pltpu.CompilerParams(dimension_semantics=("parallel", "parallel"),
                                             vmem_limit_bytes=VMEM_LIMIT),
        name="attention",
    )(q, k, v)


def _outproj_kernel(attn_ref, sgu_ref, x_ref, wo_ref, g_ref, b_ref, wr_ref, br_ref,
                    h_ref, route_ref):
    mix = jnp.dot(attn_ref[...], wo_ref[:ATTN_WIDTH, :], preferred_element_type=F32)
    mix = mix + jnp.dot(sgu_ref[...], wo_ref[ATTN_WIDTH:, :], preferred_element_type=F32)
    h = _layer_norm(ALPHA * x_ref[...] + mix, g_ref[...], b_ref[...])
    h_ref[...] = h

    h_hi = h.astype(BF16)
    h_lo = (h - h_hi.astype(F32)).astype(BF16)
    logits = (jnp.dot(h_hi, wr_ref[0], preferred_element_type=F32)
              + jnp.dot(h_lo, wr_ref[0], preferred_element_type=F32)
              + jnp.dot(h_hi, wr_ref[1], preferred_element_type=F32)) + br_ref[...]
    lane = lax.broadcasted_iota(jnp.int32, logits.shape, 1)
    is_group = (lane >= N_EXPERTS) & (lane < N_EXPERTS + N_GROUPS)
    gl = jnp.where(is_group, logits, NEG_INF)
    gmax = jnp.max(gl, -1, keepdims=True)
    p_sel = 1.0 / jnp.sum(jnp.where(is_group, jnp.exp(gl - gmax), 0.0), -1, keepdims=True)
    big = jnp.int32(2 ** 30)
    g_lane = jnp.min(jnp.where(is_group & (gl == gmax), lane, big), -1, keepdims=True)
    g_idx = g_lane - N_EXPERTS
    in_group = (lane >= g_idx * EXPERTS_PER_GROUP) & (lane < (g_idx + 1) * EXPERTS_PER_GROUP)
    e1 = jnp.where(in_group, logits, NEG_INF)
    v1 = jnp.max(e1, -1, keepdims=True)
    i1 = jnp.min(jnp.where(in_group & (e1 == v1), lane, big), -1, keepdims=True)
    rest = in_group & (lane != i1)
    e2 = jnp.where(rest, logits, NEG_INF)
    v2 = jnp.max(e2, -1, keepdims=True)
    i2 = jnp.min(jnp.where(rest & (e2 == v2), lane, big), -1, keepdims=True)
    t = jnp.exp(v2 - v1)
    w1 = 1.0 / (1.0 + t)
    w2 = t * w1
    route_ref[...] = jnp.where(lane == 0, i1.astype(F32),
                               jnp.where(lane == 1, i2.astype(F32),
                                         jnp.where(lane == 2, w1 * p_sel,
                                                   jnp.where(lane == 3, w2 * p_sel, 0.0))))


def _outproj(attn, sgu, x2d, w_out, ln_g, ln_b, w_r, b_r, tm=512):
    T = x2d.shape[0]
    row_blk = lambda w: pl.BlockSpec((tm, w), lambda i: (i, 0))
    full = lambda shape: pl.BlockSpec(shape, lambda i: (0,) * len(shape))
    return pl.pallas_call(
        _outproj_kernel,
        grid=(T // tm,),
        in_specs=[row_blk(ATTN_WIDTH), row_blk(SGU_WIDTH), row_blk(D_MODEL),
                  full((D_MODEL, D_MODEL)), full((1, D_MODEL)), full((1, D_MODEL)),
                  full((2, D_MODEL, LANES)), full((1, LANES))],
        out_specs=[row_blk(D_MODEL), row_blk(LANES)],
        out_shape=[jax.ShapeDtypeStruct((T, D_MODEL), F32),
                   jax.ShapeDtypeStruct((T, LANES), F32)],
        compiler_params=pltpu.CompilerParams(dimension_semantics=("parallel",),
                                             vmem_limit_bytes=VMEM_LIMIT),
        name="outproj",
    )(attn, sgu, x2d, w_out, ln_g, ln_b, w_r, b_r)


def _route_pos_kernel(route_ref, posw_ref, meta_ref, rank_scr, *, tb):
    route = route_ref[...]
    lane = lax.broadcasted_iota(jnp.int32, route.shape, 1)
    sel = [lane == route[:, k:k + 1].astype(jnp.int32) for k in range(TOP_K)]
    chosen = jnp.where(sel[0] | sel[1], 1.0, 0.0).astype(BF16)
    chunk = 2 * LANES
    ri = lax.broadcasted_iota(jnp.int32, (chunk, chunk), 0)
    ci = lax.broadcasted_iota(jnp.int32, (chunk, chunk), 1)
    earlier = jnp.where(ri > ci, 1.0, 0.0).astype(BF16)
    count = jnp.zeros((1, LANES), F32)
    for ch in range(tb // chunk):
        c = chosen[ch * chunk:(ch + 1) * chunk, :]
        rank_scr[ch * chunk:(ch + 1) * chunk, :] = jnp.dot(earlier, c, preferred_element_type=F32) + count
        count = count + jnp.sum(c.astype(F32), axis=0, keepdims=True)
    padded = jnp.floor((count + (MOE_PAD - 1)) / MOE_PAD) * MOE_PAD
    li = lax.broadcasted_iota(jnp.int32, (LANES, LANES), 0)
    lj = lax.broadcasted_iota(jnp.int32, (LANES, LANES), 1)
    before = jnp.where(li < lj, 1.0, 0.0).astype(F32)
    first = jnp.dot(jnp.broadcast_to(padded, (8, LANES)), before, preferred_element_type=F32,
                    precision=lax.Precision.HIGHEST)[0:1, :]
    val = rank_scr[...] + first
    pos = [jnp.sum(jnp.where(s, val, 0.0), -1, keepdims=True) for s in sel]
    posw_ref[...] = jnp.where(lane == 0, pos[0],
                              jnp.where(lane == 1, pos[1],
                                        jnp.where((lane == 2) | (lane == 3), route, 0.0)))
    row = lax.broadcasted_iota(jnp.int32, (8, LANES), 0)
    meta_ref[...] = jnp.where(row == 0, first, jnp.where(row == 1, padded, 0.0))


def _route_pos(route, tb):
    T = route.shape[0]
    return pl.pallas_call(
        functools.partial(_route_pos_kernel, tb=tb),
        grid=(T // tb,),
        in_specs=[pl.BlockSpec((tb, LANES), lambda i: (i, 0))],
        out_specs=[pl.BlockSpec((tb, LANES), lambda i: (i, 0)), pl.BlockSpec((8, LANES), lambda i: (i, 0))],
        out_shape=[jax.ShapeDtypeStruct((T, LANES), F32),
                   jax.ShapeDtypeStruct((T // tb * 8, LANES), F32)],
        scratch_shapes=[pltpu.VMEM((tb, LANES), F32)],
        compiler_params=pltpu.CompilerParams(dimension_semantics=("parallel",)),
        name="route_pos",
    )(route)


def _moe_kernel(first_ref, rows_ref, pos_hbm, w_hbm, h_ref, wgu_ref, wd_ref, g_ref, b_ref,
                o_ref, xbuf, tmp, pos_s, w_s, sem, *, tb):
    blk = pl.program_id(0)
    e = pl.program_id(1)
    n_col = D_MODEL // LANES

    def token_rows(i):
        return pl.ds(pl.multiple_of(i * 8, 8), 8)

    @pl.when(e == 0)
    def _dispatch():
        span = pl.ds(pl.multiple_of(blk * tb * TOP_K, tb * TOP_K), tb * TOP_K)
        cp = pltpu.make_async_copy(pos_hbm.at[span], pos_s, sem.at[0])
        cw = pltpu.make_async_copy(w_hbm.at[span], w_s, sem.at[1])
        cp.start()
        cw.start()
        xbuf[...] = jnp.zeros_like(xbuf)
        cp.wait()
        cw.wait()

        def body(i, carry):
            slab = h_ref[token_rows(i), :]
            for c in range(n_col):
                tmp[c * 8:(c + 1) * 8, :] = slab[:, c * LANES:(c + 1) * LANES]
            for s in range(8):
                tile = tmp[pl.ds(s, n_col, stride=8), :]
                for k in range(TOP_K):
                    start = pl.multiple_of(pos_s[TOP_K * (i * 8 + s) + k], 8)
                    xbuf[pl.ds(start, 8), :] = tile
            return carry
        lax.fori_loop(0, tb // 8, body, 0)

    def expert_tile(row0, rows):
        base = pl.multiple_of(row0 * 8, 8 * MOE_PAD)
        cols = [pl.ds(base + c, rows, stride=8) for c in range(n_col)]
        x = jnp.concatenate([xbuf[cc, :] for cc in cols], axis=1).astype(BF16)
        gu = jnp.dot(x, wgu_ref[0], preferred_element_type=F32)
        a, b = gu[:, :EXPERT_FF], gu[:, EXPERT_FF:]
        act = (a * jax.nn.sigmoid(a)) * b
        y = jnp.dot(act.astype(BF16), wd_ref[0], preferred_element_type=F32)
        for c, cc in enumerate(cols):
            xbuf[cc, :] = y[:, c * LANES:(c + 1) * LANES]

    first = first_ref[blk * N_EXPERTS + e]
    n_rows = rows_ref[blk * N_EXPERTS + e]
    n_big = n_rows // MOE_TILE

    def big(n, carry):
        expert_tile(first + n * MOE_TILE, MOE_TILE)
        return carry
    lax.fori_loop(0, n_big, big, 0)
    rest = first + n_big * MOE_TILE
    for size in range(MOE_PAD, MOE_TILE, MOE_PAD):
        @pl.when(n_rows % MOE_TILE == size)
        def _(size=size):
            expert_tile(rest, size)

    @pl.when(e == pl.num_programs(1) - 1)
    def _combine():
        def body(i, carry):
            for s in range(8):
                t = i * 8 + s
                acc = None
                for k in range(TOP_K):
                    start = pl.multiple_of(pos_s[TOP_K * t + k], 8)
                    term = w_s[TOP_K * t + k] * xbuf[pl.ds(start, 8), :]
                    acc = term if acc is None else acc + term
                tmp[s * 8:(s + 1) * 8, :] = acc
            for c in range(n_col):
                o_ref[token_rows(i), c * LANES:(c + 1) * LANES] = tmp[pl.ds(c, 8, stride=8), :]
            return carry
        lax.fori_loop(0, tb // 8, body, 0)

        def norm(i, carry):
            rows = pl.ds(pl.multiple_of(i * MOE_TILE, MOE_TILE), MOE_TILE)
            o_ref[rows, :] = _layer_norm(ALPHA * h_ref[rows, :] + o_ref[rows, :], g_ref[...], b_ref[...])
            return carry
        lax.fori_loop(0, tb // MOE_TILE, norm, 0)


def _moe(h, route, w_gu, w_d, ln_g, ln_b, tb=2048):
    T = h.shape[0]
    tb = min(tb, T)
    nb = T // tb
    assert T % tb == 0 and (tb * TOP_K) % 1024 == 0 and tb % MOE_TILE == 0
    posw, meta = _route_pos(route, tb)
    pos = (posw[:, :TOP_K].astype(jnp.int32) * 8).reshape(-1)
    gate = posw[:, TOP_K:2 * TOP_K].reshape(-1)
    meta = meta.reshape(nb, 8, LANES)[:, :2, :N_EXPERTS].astype(jnp.int32)
    first, n_rows = meta[:, 0].reshape(-1), meta[:, 1].reshape(-1)
    max_rows = tb * TOP_K + N_EXPERTS * MOE_PAD
    once = pl.Buffered(1)
    row_blk = lambda w: pl.BlockSpec((tb, w), lambda i, e, *_: (i, 0), pipeline_mode=once)
    full = lambda shape: pl.BlockSpec(shape, lambda i, e, *_: (0,) * len(shape))
    grid_spec = pltpu.PrefetchScalarGridSpec(
        num_scalar_prefetch=2,
        grid=(nb, N_EXPERTS),
        in_specs=[pl.BlockSpec(memory_space=pl.ANY), pl.BlockSpec(memory_space=pl.ANY),
                  row_blk(D_MODEL),
                  pl.BlockSpec((1, D_MODEL, 2 * EXPERT_FF), lambda i, e, *_: (e, 0, 0)),
                  pl.BlockSpec((1, EXPERT_FF, D_MODEL), lambda i, e, *_: (e, 0, 0)),
                  full((1, D_MODEL)), full((1, D_MODEL))],
        out_specs=row_blk(D_MODEL),
        scratch_shapes=[pltpu.VMEM((max_rows * 8, LANES), F32),
                        pltpu.VMEM((8 * 8, LANES), F32),
                        pltpu.SMEM((tb * TOP_K,), jnp.int32),
                        pltpu.SMEM((tb * TOP_K,), F32),
                        pltpu.SemaphoreType.DMA((2,))])
    return pl.pallas_call(
        functools.partial(_moe_kernel, tb=tb),
        grid_spec=grid_spec,
        out_shape=jax.ShapeDtypeStruct((T, D_MODEL), F32),
        compiler_params=pltpu.CompilerParams(dimension_semantics=("parallel", "arbitrary"),
                                             vmem_limit_bytes=VMEM_LIMIT),
        name="moe",
    )(first, n_rows, pos, gate, h, w_gu, w_d, ln_g, ln_b)


def kernel(x, positions, w_in, sgu_ln_g, sgu_ln_b, w_spatial, b_spatial, w_out, ln1_g, ln1_b,
           w_group, b_group, w_expert, b_expert, w_gate_up, w_down, ln2_g, ln2_b):
    B, S, D = x.shape
    T = B * S
    assert D == D_MODEL and S % (BLK * DILATIONS[-1]) == 0
    h = x.reshape(T, D).astype(F32)
    trig = _rope_trig(positions)
    for layer in range(DEPTH):
        w_sp = w_spatial[layer].astype(F32).reshape(N_SGU_GROUPS // 2, 2 * BLK, BLK)
        b_sp = jnp.repeat(b_spatial[layer].astype(F32).T, HEAD_DIM, axis=1)
        q, k, v, sgu = _proj(h, w_in[layer].astype(BF16), trig,
                             sgu_ln_g[layer].reshape(1, -1), sgu_ln_b[layer].reshape(1, -1),
                             w_sp, b_sp)
        attn = _attention(q, k, v, B, S)
        w_r = jnp.concatenate(
            [jnp.transpose(w_expert[layer], (1, 0, 2)).reshape(D, N_EXPERTS), w_group[layer],
             jnp.zeros((D, LANES - N_EXPERTS - N_GROUPS), F32)], -1).astype(F32)
        w_r_hi = w_r.astype(BF16)
        w_r = jnp.stack([w_r_hi, (w_r - w_r_hi.astype(F32)).astype(BF16)])
        b_r = jnp.concatenate(
            [b_expert[layer].reshape(-1), b_group[layer],
             jnp.zeros((LANES - N_EXPERTS - N_GROUPS,), F32)]).reshape(1, LANES).astype(F32)
        h1, route = _outproj(attn, sgu, h, w_out[layer].astype(BF16),
                                   ln1_g[layer].reshape(1, -1), ln1_b[layer].reshape(1, -1), w_r, b_r)
        w_gu = w_gate_up[layer].reshape(N_EXPERTS, D, 2 * EXPERT_FF).astype(BF16)
        w_d = w_down[layer].reshape(N_EXPERTS, EXPERT_FF, D).astype(BF16)
        h = _moe(h1, route, w_gu, w_d, ln2_g[layer].reshape(1, -1), ln2_b[layer].reshape(1, -1))
    return h.reshape(B, S, D).astype(x.dtype)
```

```python
import functools

import jax
import jax.numpy as jnp
from jax import lax
from jax.experimental import pallas as pl
from jax.experimental.pallas import tpu as pltpu

D_MODEL = 1024
HEAD_DIM = 64
N_HEADS = 8
ATTN_WIDTH = N_HEADS * HEAD_DIM
N_SGU_GROUPS = 8
SGU_WIDTH = 512
IN_WIDTH = 3 * ATTN_WIDTH + 2 * SGU_WIDTH
DILATIONS = (1, 4, 16)
BLK = 128
GRP = 16
ROPE_THETA = 500000.0
ROT_DIM = 16
ROT_HALF = ROT_DIM // 2
N_GROUPS = 4
EXPERTS_PER_GROUP = 8
N_EXPERTS = N_GROUPS * EXPERTS_PER_GROUP
EXPERT_FF = 256
TOP_K = 2
MOE_PAD = 64
MOE_TILE = 256
MOE_EPS = 4
LN_EPS = 1e-5
NEG_INF = -1e30
DEPTH = 1
ALPHA = (2 * DEPTH) ** 0.25

LANES = 128
VMEM_LIMIT = 56 * 1024 * 1024

F32 = jnp.float32
BF16 = jnp.bfloat16


def _layer_norm(v, g, b):
    mu = jnp.mean(v, -1, keepdims=True)
    vc = v - mu
    var = jnp.mean(vc * vc, -1, keepdims=True)
    return vc * lax.rsqrt(var + LN_EPS) * g + b


def _gelu(v):
    return 0.5 * v * (1.0 + lax.erf(v * (2.0 ** -0.5)))


def _trig_kernel(pos_ref, inv_ref, cos_ref, sin_ref):
    ang = pos_ref[...] * inv_ref[...]
    cos_ref[...] = jnp.cos(ang)
    sin_ref[...] = jnp.sin(ang)


def _rope_trig(positions):
    T = positions.size
    rows = T * ROT_HALF // LANES
    pos = jnp.repeat(positions.reshape(-1).astype(F32), ROT_HALF).reshape(rows, LANES)
    inv = ROPE_THETA ** (-jnp.arange(0, ROT_DIM, 2, dtype=F32) / ROT_DIM)
    inv = jnp.tile(inv, LANES // ROT_HALF).reshape(1, LANES)
    return pl.pallas_call(
        _trig_kernel,
        out_shape=[jax.ShapeDtypeStruct((rows, LANES), F32)] * 2,
        name="rope_trig",
    )(pos, inv)


def _proj_kernel(x_ref, w_ref, cos_ref, sin_ref, g_ref, b_ref, wsp_ref, bsp_ref,
                 q_ref, k_ref, v_ref, sgu_ref, c_tab, s1_tab, s2_tab, *, tm):
    xb = x_ref[...].astype(BF16)

    per_row = LANES // ROT_HALF
    lane = lax.broadcasted_iota(jnp.int32, (tm // per_row, LANES), 1)
    rotary = lane % HEAD_DIM < ROT_DIM
    for u in range(per_row):
        cs, sn = cos_ref[...], sin_ref[...]
        if u:
            cs = pltpu.roll(cs, LANES - ROT_HALF * u, axis=1)
            sn = pltpu.roll(sn, LANES - ROT_HALF * u, axis=1)
        cs = jnp.where(lane < ROT_HALF, cs, pltpu.roll(cs, ROT_HALF, axis=1))
        cs = jnp.where(rotary, jnp.where(lane < HEAD_DIM, cs, pltpu.roll(cs, HEAD_DIM, axis=1)), 1.0)
        sn = jnp.where(lane < ROT_HALF, sn, 0.0)
        neg = -(sn + pltpu.roll(sn, HEAD_DIM, axis=1))
        rows = pl.ds(u, tm // per_row, stride=per_row)
        c_tab[rows, :] = cs
        s1_tab[rows, :] = neg
        s2_tab[rows, :] = -pltpu.roll(neg, ROT_HALF, axis=1)
    c, s1, s2 = c_tab[...], s1_tab[...], s2_tab[...]

    def rope_store(out_ref, col0, scale):
        t = jnp.dot(xb, w_ref[:, col0:col0 + ATTN_WIDTH], preferred_element_type=F32)
        for j in range(ATTN_WIDTH // LANES):
            tj = t[:, j * LANES:(j + 1) * LANES]
            up = pltpu.roll(tj, LANES - ROT_HALF, axis=1)
            dn = pltpu.roll(tj, ROT_HALF, axis=1)
            r = tj * c + up * s1 + dn * s2
            if scale != 1.0:
                r = r * scale
            out_ref[:, j * LANES:(j + 1) * LANES] = r.astype(BF16)

    rope_store(q_ref, 0, HEAD_DIM ** -0.5)
    rope_store(k_ref, ATTN_WIDTH, 1.0)
    v_ref[...] = jnp.dot(xb, w_ref[:, 2 * ATTN_WIDTH:3 * ATTN_WIDTH],
                         preferred_element_type=F32).astype(BF16)

    u = _gelu(jnp.dot(xb, w_ref[:, 3 * ATTN_WIDTH:3 * ATTN_WIDTH + SGU_WIDTH],
                      preferred_element_type=F32))
    vs = _gelu(jnp.dot(xb, w_ref[:, 3 * ATTN_WIDTH + SGU_WIDTH:], preferred_element_type=F32))
    vs = _layer_norm(vs, g_ref[...], b_ref[...]).astype(BF16)

    row = lax.broadcasted_iota(jnp.int32, (2 * BLK, BLK), 0)
    col = lax.broadcasted_iota(jnp.int32, (2 * BLK, BLK), 1)
    causal = (row % BLK) >= col
    first_group = lax.broadcasted_iota(jnp.int32, (BLK, LANES), 1) < HEAD_DIM
    for p in range(SGU_WIDTH // LANES):
        wp = jnp.where(causal, wsp_ref[p], 0.0).astype(BF16)
        for ch in range(tm // BLK):
            vblk = vs[ch * BLK:(ch + 1) * BLK, p * LANES:(p + 1) * LANES]
            z2 = jnp.dot(wp, vblk, preferred_element_type=F32)
            z = jnp.where(first_group, z2[:BLK], z2[BLK:]) + bsp_ref[:, p * LANES:(p + 1) * LANES]
            ublk = u[ch * BLK:(ch + 1) * BLK, p * LANES:(p + 1) * LANES]
            sgu_ref[ch * BLK:(ch + 1) * BLK, p * LANES:(p + 1) * LANES] = (ublk * z).astype(BF16)


def _proj(x2d, w_in, trig, sgu_g, sgu_b, w_sp, b_sp, tm=512):
    T = x2d.shape[0]
    row_blk = lambda w: pl.BlockSpec((tm, w), lambda i: (i, 0))
    full = lambda shape: pl.BlockSpec(shape, lambda i: (0,) * len(shape))
    trig_blk = pl.BlockSpec((tm * ROT_HALF // LANES, LANES), lambda i: (i, 0))
    out = jax.ShapeDtypeStruct((T, ATTN_WIDTH), BF16)
    return pl.pallas_call(
        functools.partial(_proj_kernel, tm=tm),
        grid=(T // tm,),
        in_specs=[row_blk(D_MODEL), full((D_MODEL, IN_WIDTH)), trig_blk, trig_blk,
                  full((1, SGU_WIDTH)), full((1, SGU_WIDTH)),
                  full((N_SGU_GROUPS // 2, 2 * BLK, BLK)), full((BLK, SGU_WIDTH))],
        out_specs=[row_blk(ATTN_WIDTH)] * 4,
        out_shape=[out] * 4,
        scratch_shapes=[pltpu.VMEM((tm, LANES), F32)] * 3,
        compiler_params=pltpu.CompilerParams(dimension_semantics=("parallel",),
                                             vmem_limit_bytes=VMEM_LIMIT),
        name="proj",
    )(x2d, w_in, *trig, sgu_g, sgu_b, w_sp, b_sp)


_NT = (((1,), (1,)), ((), ()))


def _attn_kernel(q_ref, k_ref, v_ref, o_ref, nat, t4, q4, k4, va1, va4, q16, k16, va16,
                 s_scr, p_scr, mb_scr, st_a, st_b, *, seq):
    nblk = seq // BLK
    n_grp = nblk // GRP
    quarter = seq // 4
    qi = lax.broadcasted_iota(jnp.int32, (2 * BLK, BLK), 0) % BLK
    kj = lax.broadcasted_iota(jnp.int32, (2 * BLK, BLK), 1)
    cur_bias = jnp.where(kj <= qi, 0.0, NEG_INF).astype(F32)
    prev_bias = jnp.where(kj >= qi, 0.0, NEG_INF).astype(F32)
    both_bias = jnp.concatenate([prev_bias, cur_bias], axis=1)
    head0 = lax.broadcasted_iota(jnp.int32, (BLK, LANES), 1) < HEAD_DIM

    ones = jnp.ones((seq, LANES), BF16)
    va1[:, LANES:] = ones
    va4[:, LANES:] = ones
    va16[:, LANES:] = ones
    va1[:, :LANES] = v_ref[...]
    nat[0] = q_ref[...].astype(F32)
    nat[1] = k_ref[...].astype(F32)
    nat[2] = v_ref[...].astype(F32)

    def split4(src, dst_f32, dsts):
        def body(r, carry):
            base = pl.multiple_of(r * quarter, quarter)
            for i in range(3):
                for n in range(quarter // BLK):
                    chunk = src[i, pl.ds(r + n * 4 * BLK, BLK, stride=4), :]
                    rows = pl.ds(base + n * BLK, BLK)
                    if dst_f32 is not None:
                        dst_f32[i, rows, :] = chunk
                    dsts[i][rows, :LANES] = chunk.astype(BF16)
            return carry
        lax.fori_loop(0, 4, body, 0)

    split4(nat, t4, (q4, k4, va4))
    split4(t4, None, (q16, k16, va16))

    def regroup(src, dst):
        def body(r, carry):
            base = pl.multiple_of(r * quarter, quarter)
            for i in range(3):
                for n in range(quarter // BLK):
                    dst[i, pl.ds(r + n * 4 * BLK, BLK, stride=4), :] = src[i, pl.ds(base + n * BLK, BLK), :]
            return carry
        lax.fori_loop(0, 4, body, 0)

    def group(b0, prev_flags, q_src, k_src, va_src, st, first):
        def rows(j, back=0, n=1):
            start = b0 * BLK + (j - back) * BLK
            if not isinstance(start, int):
                start = pl.multiple_of(start, BLK)
            return pl.ds(start, n * BLK)

        for j, has_prev in enumerate(prev_flags):
            qb = q_src[rows(j), :]
            zero = jnp.zeros_like(qb)
            q2 = jnp.concatenate([jnp.where(head0, qb, zero), jnp.where(head0, zero, qb)], axis=0)
            kk = k_src[rows(j, 1, 2), :] if has_prev else k_src[rows(j), :]
            s = lax.dot_general(q2, kk, _NT, preferred_element_type=F32)
            if has_prev:
                s_scr[j] = s
            else:
                s_scr[j, :, BLK:] = s
        for j, has_prev in enumerate(prev_flags):
            s = (s_scr[j] + both_bias) if has_prev else (s_scr[j, :, BLK:] + cur_bias)
            m = jnp.max(s, -1, keepdims=True)
            p = jnp.exp(s - m).astype(BF16)
            if has_prev:
                p_scr[j] = p
            else:
                p_scr[j, :, BLK:] = p
            mb_scr[j] = jnp.broadcast_to(m, (2 * BLK, LANES))
        for j, has_prev in enumerate(prev_flags):
            if has_prev:
                res = jnp.dot(p_scr[j], va_src[rows(j, 1, 2), :], preferred_element_type=F32)
            else:
                res = jnp.dot(p_scr[j, :, BLK:], va_src[rows(j), :], preferred_element_type=F32)
            acc = jnp.where(head0, res[:BLK, :LANES], res[BLK:, :LANES])
            l = jnp.where(head0, res[:BLK, LANES:], res[BLK:, LANES:])
            m = jnp.where(head0, mb_scr[j, :BLK, :], mb_scr[j, BLK:, :])
            idx = rows(j)
            if first:
                st[0, idx, :] = m
                st[1, idx, :] = l
                st[2, idx, :] = acc
            else:
                mo, lo, ao = st[0, idx, :], st[1, idx, :], st[2, idx, :]
                mn = jnp.maximum(mo, m)
                wo = jnp.exp(mo - mn)
                wn = jnp.exp(m - mn)
                st[0, idx, :] = mn
                st[1, idx, :] = wo * lo + wn * l
                st[2, idx, :] = wo * ao + wn * acc

    def pattern(seg, q_src, k_src, va_src, st, first):
        def flags(g):
            return tuple((g * GRP + j) % seg != 0 for j in range(GRP))
        start = 0
        while start < n_grp and flags(start) != flags(n_grp - 1):
            group(start * GRP, flags(start), q_src, k_src, va_src, st, first)
            start += 1

        def body(g, carry):
            group(g * GRP, flags(n_grp - 1), q_src, k_src, va_src, st, first)
            return carry
        lax.fori_loop(start, n_grp, body, 0)

    pattern(nblk // 16, q16, k16, va16, st_a, True)
    regroup(st_a, st_b)
    pattern(nblk // 4, q4, k4, va4, st_b, False)
    regroup(st_b, st_a)
    pattern(nblk, q_ref, k_ref, va1, st_a, False)
    o_ref[...] = (st_a[2] / st_a[1]).astype(BF16)


def _attention(q, k, v, batch, seq):
    T = q.shape[0]
    assert seq % (16 * BLK) == 0 and (seq // BLK) % GRP == 0 and DILATIONS == (1, 4, 16)
    blk = pl.BlockSpec((seq, LANES), lambda b, p: (b, p))
    half = pltpu.VMEM((seq, LANES), BF16)
    wide = pltpu.VMEM((seq, 2 * LANES), BF16)
    state = pltpu.VMEM((3, seq, LANES), F32)
    scratch = [state, state,
               half, half, wide, wide, half, half, wide,
               pltpu.VMEM((GRP, 2 * BLK, 2 * BLK), F32),
               pltpu.VMEM((GRP, 2 * BLK, 2 * BLK), BF16),
               pltpu.VMEM((GRP, 2 * BLK, LANES), F32),
               state, state]
    return pl.pallas_call(
        functools.partial(_attn_kernel, seq=seq),
        grid=(batch, ATTN_WIDTH // LANES),
        in_specs=[blk, blk, blk],
        out_specs=blk,
        out_shape=jax.ShapeDtypeStruct((T, ATTN_WIDTH), BF16),
        scratch_shapes=scratch,
        compiler_params=pltpu.CompilerParams(dimension_semantics=("parallel", "parallel"),
                                             vmem_limit_bytes=VMEM_LIMIT),
        name="attention",
    )(q, k, v)


def _outproj_kernel(attn_ref, sgu_ref, x_ref, wo_ref, g_ref, b_ref, wr_ref, br_ref,
                    h_ref, route_ref):
    mix = jnp.dot(attn_ref[...], wo_ref[:ATTN_WIDTH, :], preferred_element_type=F32)
    mix = mix + jnp.dot(sgu_ref[...], wo_ref[ATTN_WIDTH:, :], preferred_element_type=F32)
    h = _layer_norm(ALPHA * x_ref[...] + mix, g_ref[...], b_ref[...])
    h_ref[...] = h

    h_hi = h.astype(BF16)
    h_lo = (h - h_hi.astype(F32)).astype(BF16)
    logits = (jnp.dot(h_hi, wr_ref[0], preferred_element_type=F32)
              + jnp.dot(h_lo, wr_ref[0], preferred_element_type=F32)
              + jnp.dot(h_hi, wr_ref[1], preferred_element_type=F32)) + br_ref[...]
    lane = lax.broadcasted_iota(jnp.int32, logits.shape, 1)
    is_group = (lane >= N_EXPERTS) & (lane < N_EXPERTS + N_GROUPS)
    gl = jnp.where(is_group, logits, NEG_INF)
    gmax = jnp.max(gl, -1, keepdims=True)
    p_sel = 1.0 / jnp.sum(jnp.where(is_group, jnp.exp(gl - gmax), 0.0), -1, keepdims=True)
    big = jnp.int32(2 ** 30)
    g_lane = jnp.min(jnp.where(is_group & (gl == gmax), lane, big), -1, keepdims=True)
    g_idx = g_lane - N_EXPERTS
    in_group = (lane >= g_idx * EXPERTS_PER_GROUP) & (lane < (g_idx + 1) * EXPERTS_PER_GROUP)
    e1 = jnp.where(in_group, logits, NEG_INF)
    v1 = jnp.max(e1, -1, keepdims=True)
    i1 = jnp.min(jnp.where(in_group & (e1 == v1), lane, big), -1, keepdims=True)
    rest = in_group & (lane != i1)
    e2 = jnp.where(rest, logits, NEG_INF)
    v2 = jnp.max(e2, -1, keepdims=True)
    i2 = jnp.min(jnp.where(rest & (e2 == v2), lane, big), -1, keepdims=True)
    t = jnp.exp(v2 - v1)
    w1 = 1.0 / (1.0 + t)
    w2 = t * w1
    route_ref[...] = jnp.where(lane == 0, i1.astype(F32),
                               jnp.where(lane == 1, i2.astype(F32),
                                         jnp.where(lane == 2, w1 * p_sel,
                                                   jnp.where(lane == 3, w2 * p_sel, 0.0))))


def _outproj(attn, sgu, x2d, w_out, ln_g, ln_b, w_r, b_r, tm=512):
    T = x2d.shape[0]
    row_blk = lambda w: pl.BlockSpec((tm, w), lambda i: (i, 0))
    full = lambda shape: pl.BlockSpec(shape, lambda i: (0,) * len(shape))
    return pl.pallas_call(
        _outproj_kernel,
        grid=(T // tm,),
        in_specs=[row_blk(ATTN_WIDTH), row_blk(SGU_WIDTH), row_blk(D_MODEL),
                  full((D_MODEL, D_MODEL)), full((1, D_MODEL)), full((1, D_MODEL)),
                  full((2, D_MODEL, LANES)), full((1, LANES))],
        out_specs=[row_blk(D_MODEL), row_blk(LANES)],
        out_shape=[jax.ShapeDtypeStruct((T, D_MODEL), F32),
                   jax.ShapeDtypeStruct((T, LANES), F32)],
        compiler_params=pltpu.CompilerParams(dimension_semantics=("parallel",),
                                             vmem_limit_bytes=VMEM_LIMIT),
        name="outproj",
    )(attn, sgu, x2d, w_out, ln_g, ln_b, w_r, b_r)


def _route_pos_kernel(route_ref, posw_ref, meta_ref, rank_scr, *, tb):
    route = route_ref[...]
    lane = lax.broadcasted_iota(jnp.int32, route.shape, 1)
    sel = [lane == route[:, k:k + 1].astype(jnp.int32) for k in range(TOP_K)]
    chosen = jnp.where(sel[0] | sel[1], 1.0, 0.0).astype(BF16)
    chunk = 2 * LANES
    ri = lax.broadcasted_iota(jnp.int32, (chunk, chunk), 0)
    ci = lax.broadcasted_iota(jnp.int32, (chunk, chunk), 1)
    earlier = jnp.where(ri > ci, 1.0, 0.0).astype(BF16)
    count = jnp.zeros((1, LANES), F32)
    for ch in range(tb // chunk):
        c = chosen[ch * chunk:(ch + 1) * chunk, :]
        rank_scr[ch * chunk:(ch + 1) * chunk, :] = jnp.dot(earlier, c, preferred_element_type=F32) + count
        count = count + jnp.sum(c.astype(F32), axis=0, keepdims=True)
    padded = jnp.floor((count + (MOE_PAD - 1)) / MOE_PAD) * MOE_PAD
    li = lax.broadcasted_iota(jnp.int32, (LANES, LANES), 0)
    lj = lax.broadcasted_iota(jnp.int32, (LANES, LANES), 1)
    before = jnp.where(li < lj, 1.0, 0.0).astype(F32)
    first = jnp.dot(jnp.broadcast_to(padded, (8, LANES)), before, preferred_element_type=F32,
                    precision=lax.Precision.HIGHEST)[0:1, :]
    val = rank_scr[...] + first
    pos = [jnp.sum(jnp.where(s, val, 0.0), -1, keepdims=True) for s in sel]
    posw_ref[...] = jnp.where(lane == 0, pos[0],
                              jnp.where(lane == 1, pos[1],
                                        jnp.where((lane == 2) | (lane == 3), route, 0.0)))
    row = lax.broadcasted_iota(jnp.int32, (8, LANES), 0)
    meta_ref[...] = jnp.where(row == 0, first, jnp.where(row == 1, padded, 0.0))


def _route_pos(route, tb):
    T = route.shape[0]
    return pl.pallas_call(
        functools.partial(_route_pos_kernel, tb=tb),
        grid=(T // tb,),
        in_specs=[pl.BlockSpec((tb, LANES), lambda i: (i, 0))],
        out_specs=[pl.BlockSpec((tb, LANES), lambda i: (i, 0)), pl.BlockSpec((8, LANES), lambda i: (i, 0))],
        out_shape=[jax.ShapeDtypeStruct((T, LANES), F32),
                   jax.ShapeDtypeStruct((T // tb * 8, LANES), F32)],
        scratch_shapes=[pltpu.VMEM((tb, LANES), F32)],
        compiler_params=pltpu.CompilerParams(dimension_semantics=("parallel",)),
        name="route_pos",
    )(route)


def _moe_kernel(first_ref, rows_ref, pos_hbm, w_hbm, h_ref, wgu_ref, wd_ref, g_ref, b_ref,
                o_ref, xbuf, tmp, pos_s, w_s, sem, *, tb):
    blk = pl.program_id(0)
    e = pl.program_id(1)
    n_col = D_MODEL // LANES

    def token_rows(i):
        return pl.ds(pl.multiple_of(i * 8, 8), 8)

    @pl.when(e == 0)
    def _dispatch():
        span = pl.ds(pl.multiple_of(blk * tb * TOP_K, tb * TOP_K), tb * TOP_K)
        cp = pltpu.make_async_copy(pos_hbm.at[span], pos_s, sem.at[0])
        cw = pltpu.make_async_copy(w_hbm.at[span], w_s, sem.at[1])
        cp.start()
        cw.start()
        xbuf[...] = jnp.zeros_like(xbuf)
        cp.wait()
        cw.wait()

        def body(i, carry):
            slab = h_ref[token_rows(i), :]
            for c in range(n_col):
                tmp[c * 8:(c + 1) * 8, :] = slab[:, c * LANES:(c + 1) * LANES]
            for s in range(8):
                tile = tmp[pl.ds(s, n_col, stride=8), :]
                for k in range(TOP_K):
                    start = pl.multiple_of(pos_s[TOP_K * (i * 8 + s) + k], 8)
                    xbuf[pl.ds(start, 8), :] = tile
            return carry
        lax.fori_loop(0, tb // 8, body, 0)

    def expert_tile(j, row0, rows):
        base = pl.multiple_of(row0 * 8, 8 * MOE_PAD)
        cols = [pl.ds(base + c, rows, stride=8) for c in range(n_col)]
        x = jnp.concatenate([xbuf[cc, :] for cc in cols], axis=1).astype(BF16)
        gu = jnp.dot(x, wgu_ref[j], preferred_element_type=F32)
        a, b = gu[:, :EXPERT_FF], gu[:, EXPERT_FF:]
        act = (a * jax.nn.sigmoid(a)) * b
        y = jnp.dot(act.astype(BF16), wd_ref[j], preferred_element_type=F32)
        for c, cc in enumerate(cols):
            xbuf[cc, :] = y[:, c * LANES:(c + 1) * LANES]

    for j in range(MOE_EPS):
        seg = blk * N_EXPERTS + e * MOE_EPS + j
        first = first_ref[seg]
        n_rows = rows_ref[seg]
        n_big = n_rows // MOE_TILE

        def big(n, carry, j=j, first=first):
            expert_tile(j, first + n * MOE_TILE, MOE_TILE)
            return carry
        lax.fori_loop(0, n_big, big, 0)
        rest = first + n_big * MOE_TILE
        for size in range(MOE_PAD, MOE_TILE, MOE_PAD):
            @pl.when(n_rows % MOE_TILE == size)
            def _(j=j, rest=rest, size=size):
                expert_tile(j, rest, size)

    @pl.when(e == pl.num_programs(1) - 1)
    def _combine():
        def body(i, carry):
            for s in range(8):
                t = i * 8 + s
                acc = None
                for k in range(TOP_K):
                    start = pl.multiple_of(pos_s[TOP_K * t + k], 8)
                    term = w_s[TOP_K * t + k] * xbuf[pl.ds(start, 8), :]
                    acc = term if acc is None else acc + term
                tmp[s * 8:(s + 1) * 8, :] = acc
            for c in range(n_col):
                o_ref[token_rows(i), c * LANES:(c + 1) * LANES] = tmp[pl.ds(c, 8, stride=8), :]
            return carry
        lax.fori_loop(0, tb // 8, body, 0)

        def norm(i, carry):
            rows = pl.ds(pl.multiple_of(i * MOE_TILE, MOE_TILE), MOE_TILE)
            o_ref[rows, :] = _layer_norm(ALPHA * h_ref[rows, :] + o_ref[rows, :], g_ref[...], b_ref[...])
            return carry
        lax.fori_loop(0, tb // MOE_TILE, norm, 0)


def _moe(h, route, w_gu, w_d, ln_g, ln_b, tb=2048):
    T = h.shape[0]
    tb = min(tb, T)
    nb = T // tb
    assert T % tb == 0 and (tb * TOP_K) % 1024 == 0 and tb % MOE_TILE == 0
    posw, meta = _route_pos(route, tb)
    pos = (posw[:, :TOP_K].astype(jnp.int32) * 8).reshape(-1)
    gate = posw[:, TOP_K:2 * TOP_K].reshape(-1)
    meta = meta.reshape(nb, 8, LANES)[:, :2, :N_EXPERTS].astype(jnp.int32)
    first, n_rows = meta[:, 0].reshape(-1), meta[:, 1].reshape(-1)
    max_rows = tb * TOP_K + N_EXPERTS * MOE_PAD
    once = pl.Buffered(1)
    row_blk = lambda w: pl.BlockSpec((tb, w), lambda i, e, *_: (i, 0), pipeline_mode=once)
    full = lambda shape: pl.BlockSpec(shape, lambda i, e, *_: (0,) * len(shape))
    grid_spec = pltpu.PrefetchScalarGridSpec(
        num_scalar_prefetch=2,
        grid=(nb, N_EXPERTS // MOE_EPS),
        in_specs=[pl.BlockSpec(memory_space=pl.ANY), pl.BlockSpec(memory_space=pl.ANY),
                  row_blk(D_MODEL),
                  pl.BlockSpec((MOE_EPS, D_MODEL, 2 * EXPERT_FF), lambda i, e, *_: (e, 0, 0)),
                  pl.BlockSpec((MOE_EPS, EXPERT_FF, D_MODEL), lambda i, e, *_: (e, 0, 0)),
                  full((1, D_MODEL)), full((1, D_MODEL))],
        out_specs=row_blk(D_MODEL),
        scratch_shapes=[pltpu.VMEM((max_rows * 8, LANES), F32),
                        pltpu.VMEM((8 * 8, LANES), F32),
                        pltpu.SMEM((tb * TOP_K,), jnp.int32),
                        pltpu.SMEM((tb * TOP_K,), F32),
                        pltpu.SemaphoreType.DMA((2,))])
    return pl.pallas_call(
        functools.partial(_moe_kernel, tb=tb),
        grid_spec=grid_spec,
        out_shape=jax.ShapeDtypeStruct((T, D_MODEL), F32),
        compiler_params=pltpu.CompilerParams(dimension_semantics=("parallel", "arbitrary"),
                                             vmem_limit_bytes=VMEM_LIMIT),
        name="moe",
    )(first, n_rows, pos, gate, h, w_gu, w_d, ln_g, ln_b)


def kernel(x, positions, w_in, sgu_ln_g, sgu_ln_b, w_spatial, b_spatial, w_out, ln1_g, ln1_b,
           w_group, b_group, w_expert, b_expert, w_gate_up, w_down, ln2_g, ln2_b):
    B, S, D = x.shape
    T = B * S
    assert D == D_MODEL and S % (BLK * DILATIONS[-1]) == 0
    h = x.reshape(T, D).astype(F32)
    trig = _rope_trig(positions)
    for layer in range(DEPTH):
        w_sp = w_spatial[layer].astype(F32).reshape(N_SGU_GROUPS // 2, 2 * BLK, BLK)
        b_sp = jnp.repeat(b_spatial[layer].astype(F32).T, HEAD_DIM, axis=1)
        q, k, v, sgu = _proj(h, w_in[layer].astype(BF16), trig,
                             sgu_ln_g[layer].reshape(1, -1), sgu_ln_b[layer].reshape(1, -1),
                             w_sp, b_sp)
        attn = _attention(q, k, v, B, S)
        w_r = jnp.concatenate(
            [jnp.transpose(w_expert[layer], (1, 0, 2)).reshape(D, N_EXPERTS), w_group[layer],
             jnp.zeros((D, LANES - N_EXPERTS - N_GROUPS), F32)], -1).astype(F32)
        w_r_hi = w_r.astype(BF16)
        w_r = jnp.stack([w_r_hi, (w_r - w_r_hi.astype(F32)).astype(BF16)])
        b_r = jnp.concatenate(
            [b_expert[layer].reshape(-1), b_group[layer],
             jnp.zeros((LANES - N_EXPERTS - N_GROUPS,), F32)]).reshape(1, LANES).astype(F32)
        h1, route = _outproj(attn, sgu, h, w_out[layer].astype(BF16),
                                   ln1_g[layer].reshape(1, -1), ln1_b[layer].reshape(1, -1), w_r, b_r)
        w_gu = w_gate_up[layer].reshape(N_EXPERTS, D, 2 * EXPERT_FF).astype(BF16)
        w_d = w_down[layer].reshape(N_EXPERTS, EXPERT_FF, D).astype(BF16)
        h = _moe(h1, route, w_gu, w_d, ln2_g[layer].reshape(1, -1), ln2_b[layer].reshape(1, -1))
    return h.reshape(B, S, D).astype(x.dtype)
```

```python
import functools

import jax
import jax.numpy as jnp
from jax import lax
from jax.experimental import pallas as pl
from jax.experimental.pallas import tpu as pltpu

D_MODEL = 1024
HEAD_DIM = 64
N_HEADS = 8
ATTN_WIDTH = N_HEADS * HEAD_DIM
N_SGU_GROUPS = 8
SGU_WIDTH = 512
IN_WIDTH = 3 * ATTN_WIDTH + 2 * SGU_WIDTH
DILATIONS = (1, 4, 16)
BLK = 128
GRP = 16
ROPE_THETA = 500000.0
ROT_DIM = 16
ROT_HALF = ROT_DIM // 2
N_GROUPS = 4
EXPERTS_PER_GROUP = 8
N_EXPERTS = N_GROUPS * EXPERTS_PER_GROUP
EXPERT_FF = 256
TOP_K = 2
MOE_PAD = 64
MOE_TILE = 256
MOE_EPS = 4
MOE_SLABS = 2
LN_EPS = 1e-5
NEG_INF = -1e30
DEPTH = 1
ALPHA = (2 * DEPTH) ** 0.25

LANES = 128
VMEM_LIMIT = 58 * 1024 * 1024

F32 = jnp.float32
BF16 = jnp.bfloat16


def _layer_norm(v, g, b):
    mu = jnp.mean(v, -1, keepdims=True)
    vc = v - mu
    var = jnp.mean(vc * vc, -1, keepdims=True)
    return vc * lax.rsqrt(var + LN_EPS) * g + b


def _gelu(v):
    return 0.5 * v * (1.0 + lax.erf(v * (2.0 ** -0.5)))


def _trig_kernel(pos_ref, inv_ref, cos_ref, sin_ref):
    ang = pos_ref[...] * inv_ref[...]
    cos_ref[...] = jnp.cos(ang)
    sin_ref[...] = jnp.sin(ang)


def _rope_trig(positions):
    T = positions.size
    rows = T * ROT_HALF // LANES
    pos = jnp.repeat(positions.reshape(-1).astype(F32), ROT_HALF).reshape(rows, LANES)
    inv = ROPE_THETA ** (-jnp.arange(0, ROT_DIM, 2, dtype=F32) / ROT_DIM)
    inv = jnp.tile(inv, LANES // ROT_HALF).reshape(1, LANES)
    return pl.pallas_call(
        _trig_kernel,
        out_shape=[jax.ShapeDtypeStruct((rows, LANES), F32)] * 2,
        name="rope_trig",
    )(pos, inv)


def _proj_kernel(x_ref, w_ref, cos_ref, sin_ref, g_ref, b_ref, wsp_ref, bsp_ref,
                 q_ref, k_ref, v_ref, sgu_ref, c_tab, s1_tab, s2_tab, *, tm):
    xb = x_ref[...].astype(BF16)

    per_row = LANES // ROT_HALF
    lane = lax.broadcasted_iota(jnp.int32, (tm // per_row, LANES), 1)
    rotary = lane % HEAD_DIM < ROT_DIM
    for u in range(per_row):
        cs, sn = cos_ref[...], sin_ref[...]
        if u:
            cs = pltpu.roll(cs, LANES - ROT_HALF * u, axis=1)
            sn = pltpu.roll(sn, LANES - ROT_HALF * u, axis=1)
        cs = jnp.where(lane < ROT_HALF, cs, pltpu.roll(cs, ROT_HALF, axis=1))
        cs = jnp.where(rotary, jnp.where(lane < HEAD_DIM, cs, pltpu.roll(cs, HEAD_DIM, axis=1)), 1.0)
        sn = jnp.where(lane < ROT_HALF, sn, 0.0)
        neg = -(sn + pltpu.roll(sn, HEAD_DIM, axis=1))
        rows = pl.ds(u, tm // per_row, stride=per_row)
        c_tab[rows, :] = cs
        s1_tab[rows, :] = neg
        s2_tab[rows, :] = -pltpu.roll(neg, ROT_HALF, axis=1)
    c, s1, s2 = c_tab[...], s1_tab[...], s2_tab[...]

    def rope_store(out_ref, col0, scale):
        t = jnp.dot(xb, w_ref[:, col0:col0 + ATTN_WIDTH], preferred_element_type=F32)
        for j in range(ATTN_WIDTH // LANES):
            tj = t[:, j * LANES:(j + 1) * LANES]
            up = pltpu.roll(tj, LANES - ROT_HALF, axis=1)
            dn = pltpu.roll(tj, ROT_HALF, axis=1)
            r = tj * c + up * s1 + dn * s2
            if scale != 1.0:
                r = r * scale
            out_ref[:, j * LANES:(j + 1) * LANES] = r.astype(BF16)

    rope_store(q_ref, 0, HEAD_DIM ** -0.5)
    rope_store(k_ref, ATTN_WIDTH, 1.0)
    v_ref[...] = jnp.dot(xb, w_ref[:, 2 * ATTN_WIDTH:3 * ATTN_WIDTH],
                         preferred_element_type=F32).astype(BF16)

    u = _gelu(jnp.dot(xb, w_ref[:, 3 * ATTN_WIDTH:3 * ATTN_WIDTH + SGU_WIDTH],
                      preferred_element_type=F32))
    vs = _gelu(jnp.dot(xb, w_ref[:, 3 * ATTN_WIDTH + SGU_WIDTH:], preferred_element_type=F32))
    vs = _layer_norm(vs, g_ref[...], b_ref[...]).astype(BF16)

    row = lax.broadcasted_iota(jnp.int32, (2 * BLK, BLK), 0)
    col = lax.broadcasted_iota(jnp.int32, (2 * BLK, BLK), 1)
    causal = (row % BLK) >= col
    first_group = lax.broadcasted_iota(jnp.int32, (BLK, LANES), 1) < HEAD_DIM
    for p in range(SGU_WIDTH // LANES):
        wp = jnp.where(causal, wsp_ref[p], 0.0).astype(BF16)
        for ch in range(tm // BLK):
            vblk = vs[ch * BLK:(ch + 1) * BLK, p * LANES:(p + 1) * LANES]
            z2 = jnp.dot(wp, vblk, preferred_element_type=F32)
            z = jnp.where(first_group, z2[:BLK], z2[BLK:]) + bsp_ref[:, p * LANES:(p + 1) * LANES]
            ublk = u[ch * BLK:(ch + 1) * BLK, p * LANES:(p + 1) * LANES]
            sgu_ref[ch * BLK:(ch + 1) * BLK, p * LANES:(p + 1) * LANES] = (ublk * z).astype(BF16)


def _proj(x2d, w_in, trig, sgu_g, sgu_b, w_sp, b_sp, tm=512):
    T = x2d.shape[0]
    row_blk = lambda w: pl.BlockSpec((tm, w), lambda i: (i, 0))
    full = lambda shape: pl.BlockSpec(shape, lambda i: (0,) * len(shape))
    trig_blk = pl.BlockSpec((tm * ROT_HALF // LANES, LANES), lambda i: (i, 0))
    out = jax.ShapeDtypeStruct((T, ATTN_WIDTH), BF16)
    return pl.pallas_call(
        functools.partial(_proj_kernel, tm=tm),
        grid=(T // tm,),
        in_specs=[row_blk(D_MODEL), full((D_MODEL, IN_WIDTH)), trig_blk, trig_blk,
                  full((1, SGU_WIDTH)), full((1, SGU_WIDTH)),
                  full((N_SGU_GROUPS // 2, 2 * BLK, BLK)), full((BLK, SGU_WIDTH))],
        out_specs=[row_blk(ATTN_WIDTH)] * 4,
        out_shape=[out] * 4,
        scratch_shapes=[pltpu.VMEM((tm, LANES), F32)] * 3,
        compiler_params=pltpu.CompilerParams(dimension_semantics=("parallel",),
                                             vmem_limit_bytes=VMEM_LIMIT),
        name="proj",
    )(x2d, w_in, *trig, sgu_g, sgu_b, w_sp, b_sp)


_NT = (((1,), (1,)), ((), ()))


def _attn_kernel(q_ref, k_ref, v_ref, o_ref, nat, t4, q4, k4, va1, va4, q16, k16, va16,
                 s_scr, p_scr, mb_scr, st_a, st_b, *, seq):
    nblk = seq // BLK
    n_grp = nblk // GRP
    quarter = seq // 4
    qi = lax.broadcasted_iota(jnp.int32, (2 * BLK, BLK), 0) % BLK
    kj = lax.broadcasted_iota(jnp.int32, (2 * BLK, BLK), 1)
    cur_bias = jnp.where(kj <= qi, 0.0, NEG_INF).astype(F32)
    prev_bias = jnp.where(kj >= qi, 0.0, NEG_INF).astype(F32)
    both_bias = jnp.concatenate([prev_bias, cur_bias], axis=1)
    head0 = lax.broadcasted_iota(jnp.int32, (BLK, LANES), 1) < HEAD_DIM

    ones = jnp.ones((seq, LANES), BF16)
    va1[:, LANES:] = ones
    va4[:, LANES:] = ones
    va16[:, LANES:] = ones
    va1[:, :LANES] = v_ref[...]
    nat[0] = q_ref[...].astype(F32)
    nat[1] = k_ref[...].astype(F32)
    nat[2] = v_ref[...].astype(F32)

    def split4(src, dst_f32, dsts):
        def body(r, carry):
            base = pl.multiple_of(r * quarter, quarter)
            for i in range(3):
                for n in range(quarter // BLK):
                    chunk = src[i, pl.ds(r + n * 4 * BLK, BLK, stride=4), :]
                    rows = pl.ds(base + n * BLK, BLK)
                    if dst_f32 is not None:
                        dst_f32[i, rows, :] = chunk
                    dsts[i][rows, :LANES] = chunk.astype(BF16)
            return carry
        lax.fori_loop(0, 4, body, 0)

    split4(nat, t4, (q4, k4, va4))
    split4(t4, None, (q16, k16, va16))

    def regroup(src, dst):
        def body(r, carry):
            base = pl.multiple_of(r * quarter, quarter)
            for i in range(3):
                for n in range(quarter // BLK):
                    dst[i, pl.ds(r + n * 4 * BLK, BLK, stride=4), :] = src[i, pl.ds(base + n * BLK, BLK), :]
            return carry
        lax.fori_loop(0, 4, body, 0)

    def group(b0, prev_flags, q_src, k_src, va_src, st, first):
        def rows(j, back=0, n=1):
            start = b0 * BLK + (j - back) * BLK
            if not isinstance(start, int):
                start = pl.multiple_of(start, BLK)
            return pl.ds(start, n * BLK)

        for j, has_prev in enumerate(prev_flags):
            qb = q_src[rows(j), :]
            zero = jnp.zeros_like(qb)
            q2 = jnp.concatenate([jnp.where(head0, qb, zero), jnp.where(head0, zero, qb)], axis=0)
            kk = k_src[rows(j, 1, 2), :] if has_prev else k_src[rows(j), :]
            s = lax.dot_general(q2, kk, _NT, preferred_element_type=F32)
            if has_prev:
                s_scr[j] = s
            else:
                s_scr[j, :, BLK:] = s
        for j, has_prev in enumerate(prev_flags):
            s = (s_scr[j] + both_bias) if has_prev else (s_scr[j, :, BLK:] + cur_bias)
            m = jnp.max(s, -1, keepdims=True)
            p = jnp.exp(s - m).astype(BF16)
            if has_prev:
                p_scr[j] = p
            else:
                p_scr[j, :, BLK:] = p
            mb_scr[j] = jnp.broadcast_to(m, (2 * BLK, LANES))
        for j, has_prev in enumerate(prev_flags):
            if has_prev:
                res = jnp.dot(p_scr[j], va_src[rows(j, 1, 2), :], preferred_element_type=F32)
            else:
                res = jnp.dot(p_scr[j, :, BLK:], va_src[rows(j), :], preferred_element_type=F32)
            acc = jnp.where(head0, res[:BLK, :LANES], res[BLK:, :LANES])
            l = jnp.where(head0, res[:BLK, LANES:], res[BLK:, LANES:])
            m = jnp.where(head0, mb_scr[j, :BLK, :], mb_scr[j, BLK:, :])
            idx = rows(j)
            if first:
                st[0, idx, :] = m
                st[1, idx, :] = l
                st[2, idx, :] = acc
            else:
                mo, lo, ao = st[0, idx, :], st[1, idx, :], st[2, idx, :]
                mn = jnp.maximum(mo, m)
                wo = jnp.exp(mo - mn)
                wn = jnp.exp(m - mn)
                st[0, idx, :] = mn
                st[1, idx, :] = wo * lo + wn * l
                st[2, idx, :] = wo * ao + wn * acc

    def pattern(seg, q_src, k_src, va_src, st, first):
        def flags(g):
            return tuple((g * GRP + j) % seg != 0 for j in range(GRP))
        start = 0
        while start < n_grp and flags(start) != flags(n_grp - 1):
            group(start * GRP, flags(start), q_src, k_src, va_src, st, first)
            start += 1

        def body(g, carry):
            group(g * GRP, flags(n_grp - 1), q_src, k_src, va_src, st, first)
            return carry
        lax.fori_loop(start, n_grp, body, 0)

    pattern(nblk // 16, q16, k16, va16, st_a, True)
    regroup(st_a, st_b)
    pattern(nblk // 4, q4, k4, va4, st_b, False)
    regroup(st_b, st_a)
    pattern(nblk, q_ref, k_ref, va1, st_a, False)
    o_ref[...] = (st_a[2] / st_a[1]).astype(BF16)


def _attention(q, k, v, batch, seq):
    T = q.shape[0]
    assert seq % (16 * BLK) == 0 and (seq // BLK) % GRP == 0 and DILATIONS == (1, 4, 16)
    blk = pl.BlockSpec((seq, LANES), lambda b, p: (b, p))
    half = pltpu.VMEM((seq, LANES), BF16)
    wide = pltpu.VMEM((seq, 2 * LANES), BF16)
    state = pltpu.VMEM((3, seq, LANES), F32)
    scratch = [state, state,
               half, half, wide, wide, half, half, wide,
               pltpu.VMEM((GRP, 2 * BLK, 2 * BLK), F32),
               pltpu.VMEM((GRP, 2 * BLK, 2 * BLK), BF16),
               pltpu.VMEM((GRP, 2 * BLK, LANES), F32),
               state, state]
    return pl.pallas_call(
        functools.partial(_attn_kernel, seq=seq),
        grid=(batch, ATTN_WIDTH // LANES),
        in_specs=[blk, blk, blk],
        out_specs=blk,
        out_shape=jax.ShapeDtypeStruct((T, ATTN_WIDTH), BF16),
        scratch_shapes=scratch,
        compiler_params=pltpu.CompilerParams(dimension_semantics=("parallel", "parallel"),
                                             vmem_limit_bytes=VMEM_LIMIT),
        name="attention",
    )(q, k, v)


def _outproj_kernel(attn_ref, sgu_ref, x_ref, wo_ref, g_ref, b_ref, wr_ref, br_ref,
                    h_ref, route_ref):
    mix = jnp.dot(attn_ref[...], wo_ref[:ATTN_WIDTH, :], preferred_element_type=F32)
    mix = mix + jnp.dot(sgu_ref[...], wo_ref[ATTN_WIDTH:, :], preferred_element_type=F32)
    h = _layer_norm(ALPHA * x_ref[...] + mix, g_ref[...], b_ref[...])
    h_ref[...] = h

    h_hi = h.astype(BF16)
    h_lo = (h - h_hi.astype(F32)).astype(BF16)
    both = jnp.dot(h_hi, wr_ref[...], preferred_element_type=F32)
    logits = (both[:, :LANES] + both[:, LANES:]
              + jnp.dot(h_lo, wr_ref[:, :LANES], preferred_element_type=F32)) + br_ref[...]
    lane = lax.broadcasted_iota(jnp.int32, logits.shape, 1)
    is_group = (lane >= N_EXPERTS) & (lane < N_EXPERTS + N_GROUPS)
    gl = jnp.where(is_group, logits, NEG_INF)
    gmax = jnp.max(gl, -1, keepdims=True)
    p_sel = 1.0 / jnp.sum(jnp.where(is_group, jnp.exp(gl - gmax), 0.0), -1, keepdims=True)
    big = jnp.int32(2 ** 30)
    g_lane = jnp.min(jnp.where(is_group & (gl == gmax), lane, big), -1, keepdims=True)
    g_idx = g_lane - N_EXPERTS
    in_group = (lane >= g_idx * EXPERTS_PER_GROUP) & (lane < (g_idx + 1) * EXPERTS_PER_GROUP)
    e1 = jnp.where(in_group, logits, NEG_INF)
    v1 = jnp.max(e1, -1, keepdims=True)
    i1 = jnp.min(jnp.where(in_group & (e1 == v1), lane, big), -1, keepdims=True)
    rest = in_group & (lane != i1)
    e2 = jnp.where(rest, logits, NEG_INF)
    v2 = jnp.max(e2, -1, keepdims=True)
    i2 = jnp.min(jnp.where(rest & (e2 == v2), lane, big), -1, keepdims=True)
    t = jnp.exp(v2 - v1)
    w1 = 1.0 / (1.0 + t)
    w2 = t * w1
    route_ref[...] = jnp.where(lane == 0, i1.astype(F32),
                               jnp.where(lane == 1, i2.astype(F32),
                                         jnp.where(lane == 2, w1 * p_sel,
                                                   jnp.where(lane == 3, w2 * p_sel, 0.0))))


def _outproj(attn, sgu, x2d, w_out, ln_g, ln_b, w_r, b_r, tm=512):
    T = x2d.shape[0]
    row_blk = lambda w: pl.BlockSpec((tm, w), lambda i: (i, 0))
    full = lambda shape: pl.BlockSpec(shape, lambda i: (0,) * len(shape))
    return pl.pallas_call(
        _outproj_kernel,
        grid=(T // tm,),
        in_specs=[row_blk(ATTN_WIDTH), row_blk(SGU_WIDTH), row_blk(D_MODEL),
                  full((D_MODEL, D_MODEL)), full((1, D_MODEL)), full((1, D_MODEL)),
                  full((D_MODEL, 2 * LANES)), full((1, LANES))],
        out_specs=[row_blk(D_MODEL), row_blk(LANES)],
        out_shape=[jax.ShapeDtypeStruct((T, D_MODEL), F32),
                   jax.ShapeDtypeStruct((T, LANES), F32)],
        compiler_params=pltpu.CompilerParams(dimension_semantics=("parallel",),
                                             vmem_limit_bytes=VMEM_LIMIT),
        name="outproj",
    )(attn, sgu, x2d, w_out, ln_g, ln_b, w_r, b_r)


def _route_pos_kernel(route_ref, posw_ref, meta_ref, rank_scr, *, tb):
    route = route_ref[...]
    lane = lax.broadcasted_iota(jnp.int32, route.shape, 1)
    sel = [lane == route[:, k:k + 1].astype(jnp.int32) for k in range(TOP_K)]
    chosen = jnp.where(sel[0] | sel[1], 1.0, 0.0).astype(BF16)
    chunk = 2 * LANES
    ri = lax.broadcasted_iota(jnp.int32, (chunk, chunk), 0)
    ci = lax.broadcasted_iota(jnp.int32, (chunk, chunk), 1)
    earlier = jnp.where(ri > ci, 1.0, 0.0).astype(BF16)
    count = jnp.zeros((1, LANES), F32)
    for ch in range(tb // chunk):
        c = chosen[ch * chunk:(ch + 1) * chunk, :]
        rank_scr[ch * chunk:(ch + 1) * chunk, :] = jnp.dot(earlier, c, preferred_element_type=F32) + count
        count = count + jnp.sum(c.astype(F32), axis=0, keepdims=True)
    padded = jnp.floor((count + (MOE_PAD - 1)) / MOE_PAD) * MOE_PAD
    li = lax.broadcasted_iota(jnp.int32, (LANES, LANES), 0)
    lj = lax.broadcasted_iota(jnp.int32, (LANES, LANES), 1)
    before = jnp.where(li < lj, 1.0, 0.0).astype(F32)
    first = jnp.dot(jnp.broadcast_to(padded, (8, LANES)), before, preferred_element_type=F32,
                    precision=lax.Precision.HIGHEST)[0:1, :]
    val = rank_scr[...] + first
    pos = [jnp.sum(jnp.where(s, val, 0.0), -1, keepdims=True) for s in sel]
    posw_ref[...] = jnp.where(lane == 0, pos[0],
                              jnp.where(lane == 1, pos[1],
                                        jnp.where((lane == 2) | (lane == 3), route, 0.0)))
    row = lax.broadcasted_iota(jnp.int32, (8, LANES), 0)
    meta_ref[...] = jnp.where(row == 0, first, jnp.where(row == 1, padded, 0.0))


def _route_pos(route, tb):
    T = route.shape[0]
    return pl.pallas_call(
        functools.partial(_route_pos_kernel, tb=tb),
        grid=(T // tb,),
        in_specs=[pl.BlockSpec((tb, LANES), lambda i: (i, 0))],
        out_specs=[pl.BlockSpec((tb, LANES), lambda i: (i, 0)), pl.BlockSpec((8, LANES), lambda i: (i, 0))],
        out_shape=[jax.ShapeDtypeStruct((T, LANES), F32),
                   jax.ShapeDtypeStruct((T // tb * 8, LANES), F32)],
        scratch_shapes=[pltpu.VMEM((tb, LANES), F32)],
        compiler_params=pltpu.CompilerParams(dimension_semantics=("parallel",)),
        name="route_pos",
    )(route)


def _moe_kernel(first_ref, rows_ref, pos_hbm, w_hbm, h_ref, wgu_ref, wd_ref, g_ref, b_ref,
                o_ref, xbuf, tmp, pos_s, w_s, sem, *, tb):
    blk = pl.program_id(0)
    e = pl.program_id(1)
    n_col = D_MODEL // LANES
    slab = 8 * n_col

    def token_rows(i, g):
        return pl.ds(pl.multiple_of((i * MOE_SLABS + g) * 8, 8), 8)

    @pl.when(e == 0)
    def _dispatch():
        span = pl.ds(pl.multiple_of(blk * tb * TOP_K, tb * TOP_K), tb * TOP_K)
        cp = pltpu.make_async_copy(pos_hbm.at[span], pos_s, sem.at[0])
        cw = pltpu.make_async_copy(w_hbm.at[span], w_s, sem.at[1])
        cp.start()
        cw.start()
        xbuf[...] = jnp.zeros_like(xbuf)
        cp.wait()
        cw.wait()

        def body(i, carry):
            for g in range(MOE_SLABS):
                rows = h_ref[token_rows(i, g), :]
                for c in range(n_col):
                    tmp[g * slab + c * 8:g * slab + (c + 1) * 8, :] = rows[:, c * LANES:(c + 1) * LANES]
            for g in range(MOE_SLABS):
                for s in range(8):
                    tile = tmp[pl.ds(g * slab + s, n_col, stride=8), :]
                    for k in range(TOP_K):
                        t = (i * MOE_SLABS + g) * 8 + s
                        start = pl.multiple_of(pos_s[TOP_K * t + k], 8)
                        xbuf[pl.ds(start, 8), :] = tile
            return carry
        lax.fori_loop(0, tb // (8 * MOE_SLABS), body, 0)

    def col_blocks(row0, rows):
        base = pl.multiple_of(row0 * 8, 8 * MOE_PAD)
        return [pl.ds(base + c, rows, stride=8) for c in range(n_col)]

    def expert_tile(j, row0, rows):
        cols = col_blocks(row0, rows)
        x = jnp.concatenate([xbuf[cc, :] for cc in cols], axis=1).astype(BF16)
        gu = jnp.dot(x, wgu_ref[j], preferred_element_type=F32)
        a, b = gu[:, :EXPERT_FF], gu[:, EXPERT_FF:]
        act = (a * jax.nn.sigmoid(a)) * b
        y = jnp.dot(act.astype(BF16), wd_ref[j], preferred_element_type=F32)
        for c, cc in enumerate(cols):
            xbuf[cc, :] = y[:, c * LANES:(c + 1) * LANES]

    for j in range(MOE_EPS):
        seg = blk * N_EXPERTS + e * MOE_EPS + j
        first = first_ref[seg]
        n_rows = rows_ref[seg]
        n_big = n_rows // MOE_TILE

        def big(n, carry, j=j, first=first):
            expert_tile(j, first + n * MOE_TILE, MOE_TILE)
            return carry
        lax.fori_loop(0, n_big, big, 0)
        rest = first + n_big * MOE_TILE
        for size in range(MOE_PAD, MOE_TILE, MOE_PAD):
            @pl.when(n_rows % MOE_TILE == size)
            def _(j=j, rest=rest, size=size):
                expert_tile(j, rest, size)

    @pl.when(e == pl.num_programs(1) - 1)
    def _combine():
        def body(i, carry):
            for g in range(MOE_SLABS):
                for s in range(8):
                    t = (i * MOE_SLABS + g) * 8 + s
                    acc = None
                    for k in range(TOP_K):
                        start = pl.multiple_of(pos_s[TOP_K * t + k], 8)
                        term = w_s[TOP_K * t + k] * xbuf[pl.ds(start, 8), :]
                        acc = term if acc is None else acc + term
                    tmp[g * slab + s * 8:g * slab + (s + 1) * 8, :] = acc
            for g in range(MOE_SLABS):
                for c in range(n_col):
                    o_ref[token_rows(i, g), c * LANES:(c + 1) * LANES] = tmp[pl.ds(g * slab + c, 8, stride=8), :]
            return carry
        lax.fori_loop(0, tb // (8 * MOE_SLABS), body, 0)

        def norm(i, carry):
            rows = pl.ds(pl.multiple_of(i * MOE_TILE, MOE_TILE), MOE_TILE)
            o_ref[rows, :] = _layer_norm(ALPHA * h_ref[rows, :] + o_ref[rows, :], g_ref[...], b_ref[...])
            return carry
        lax.fori_loop(0, tb // MOE_TILE, norm, 0)


def _moe(h, route, w_gu, w_d, ln_g, ln_b, tb=2048):
    T = h.shape[0]
    tb = min(tb, T)
    nb = T // tb
    assert T % tb == 0 and (tb * TOP_K) % 1024 == 0 and tb % MOE_TILE == 0
    posw, meta = _route_pos(route, tb)
    pos = (posw[:, :TOP_K].astype(jnp.int32) * 8).reshape(-1)
    gate = posw[:, TOP_K:2 * TOP_K].reshape(-1)
    meta = meta.reshape(nb, 8, LANES)[:, :2, :N_EXPERTS].astype(jnp.int32)
    first, n_rows = meta[:, 0].reshape(-1), meta[:, 1].reshape(-1)
    max_rows = tb * TOP_K + N_EXPERTS * MOE_PAD
    once = pl.Buffered(1)
    row_blk = lambda w: pl.BlockSpec((tb, w), lambda i, e, *_: (i, 0), pipeline_mode=once)
    full = lambda shape: pl.BlockSpec(shape, lambda i, e, *_: (0,) * len(shape))
    grid_spec = pltpu.PrefetchScalarGridSpec(
        num_scalar_prefetch=2,
        grid=(nb, N_EXPERTS // MOE_EPS),
        in_specs=[pl.BlockSpec(memory_space=pl.ANY), pl.BlockSpec(memory_space=pl.ANY),
                  row_blk(D_MODEL),
                  pl.BlockSpec((MOE_EPS, D_MODEL, 2 * EXPERT_FF), lambda i, e, *_: (e, 0, 0)),
                  pl.BlockSpec((MOE_EPS, EXPERT_FF, D_MODEL), lambda i, e, *_: (e, 0, 0)),
                  full((1, D_MODEL)), full((1, D_MODEL))],
        out_specs=row_blk(D_MODEL),
        scratch_shapes=[pltpu.VMEM((max_rows * 8, LANES), F32),
                        pltpu.VMEM((MOE_SLABS * 8 * 8, LANES), F32),
                        pltpu.SMEM((tb * TOP_K,), jnp.int32),
                        pltpu.SMEM((tb * TOP_K,), F32),
                        pltpu.SemaphoreType.DMA((2,))])
    return pl.pallas_call(
        functools.partial(_moe_kernel, tb=tb),
        grid_spec=grid_spec,
        out_shape=jax.ShapeDtypeStruct((T, D_MODEL), F32),
        compiler_params=pltpu.CompilerParams(dimension_semantics=("parallel", "arbitrary"),
                                             vmem_limit_bytes=VMEM_LIMIT),
        name="moe",
    )(first, n_rows, pos, gate, h, w_gu, w_d, ln_g, ln_b)


def kernel(x, positions, w_in, sgu_ln_g, sgu_ln_b, w_spatial, b_spatial, w_out, ln1_g, ln1_b,
           w_group, b_group, w_expert, b_expert, w_gate_up, w_down, ln2_g, ln2_b):
    B, S, D = x.shape
    T = B * S
    assert D == D_MODEL and S % (BLK * DILATIONS[-1]) == 0
    h = x.reshape(T, D).astype(F32)
    trig = _rope_trig(positions)
    for layer in range(DEPTH):
        w_sp = w_spatial[layer].astype(F32).reshape(N_SGU_GROUPS // 2, 2 * BLK, BLK)
        b_sp = jnp.repeat(b_spatial[layer].astype(F32).T, HEAD_DIM, axis=1)
        q, k, v, sgu = _proj(h, w_in[layer].astype(BF16), trig,
                             sgu_ln_g[layer].reshape(1, -1), sgu_ln_b[layer].reshape(1, -1),
                             w_sp, b_sp)
        attn = _attention(q, k, v, B, S)
        w_r = jnp.concatenate(
            [jnp.transpose(w_expert[layer], (1, 0, 2)).reshape(D, N_EXPERTS), w_group[layer],
             jnp.zeros((D, LANES - N_EXPERTS - N_GROUPS), F32)], -1).astype(F32)
        w_r_hi = w_r.astype(BF16)
        w_r = jnp.concatenate([w_r_hi, (w_r - w_r_hi.astype(F32)).astype(BF16)], -1)
        b_r = jnp.concatenate(
            [b_expert[layer].reshape(-1), b_group[layer],
             jnp.zeros((LANES - N_EXPERTS - N_GROUPS,), F32)]).reshape(1, LANES).astype(F32)
        h1, route = _outproj(attn, sgu, h, w_out[layer].astype(BF16),
                                   ln1_g[layer].reshape(1, -1), ln1_b[layer].reshape(1, -1), w_r, b_r)
        w_gu = w_gate_up[layer].reshape(N_EXPERTS, D, 2 * EXPERT_FF).astype(BF16)
        w_d = w_down[layer].reshape(N_EXPERTS, EXPERT_FF, D).astype(BF16)
        h = _moe(h1, route, w_gu, w_d, ln2_g[layer].reshape(1, -1), ln2_b[layer].reshape(1, -1))
    return h.reshape(B, S, D).astype(x.dtype)
```
